```python
import math
import jax, jax.numpy as jnp
from jax import lax
import numpy as np

D_MODEL = 2048
BATCH = 1
SEQ = 16384
DEPTH = 1

MOBA_HEADS = 8
MOBA_HEAD_DIM = 128
MOBA_WIDTH = MOBA_HEADS * MOBA_HEAD_DIM
MOBA_BLOCK = 256
MOBA_TOPK = 3
MOBA_Q_CHUNK = 64
PARTIAL_ROPE_DIM = MOBA_HEAD_DIM // 4
MLA_HEADS = 8
MLA_NOPE_DIM = 128
MLA_ROPE_DIM = 64
MLA_V_DIM = 128
MLA_Q_RANK = 512
MLA_KV_RANK = 256
MLA_WIDTH = MLA_HEADS * MLA_V_DIM
MLA_Q_BLOCK = 128
ROPE_THETA = 500000.0
D_FF = -(-8 * D_MODEL // (3 * 256)) * 256
EPS = 1e-6
N_BRANCHES = 2
IN_SIZES = (MOBA_WIDTH, MOBA_WIDTH, MOBA_WIDTH, MLA_Q_RANK, MLA_KV_RANK, MLA_ROPE_DIM, N_BRANCHES * D_MODEL)
IN_COLS = sum(IN_SIZES)

kernel_name = "hybrid_moba_mla_gated_block"


def rmsnorm(x, g):
    x32 = x.astype(jnp.float32)
    y = x32 * lax.rsqrt(jnp.mean(x32 * x32, axis=-1, keepdims=True) + EPS)
    return (y * g.astype(jnp.float32)).astype(x.dtype)


def rope(x, positions):
    d = x.shape[-1]
    inv_freq = ROPE_THETA ** (-jnp.arange(0, d, 2, dtype=jnp.float32) / d)
    ang = positions.astype(jnp.float32)[..., None] * inv_freq
    cos = jnp.cos(ang)[:, :, None, :]
    sin = jnp.sin(ang)[:, :, None, :]
    x32 = x.astype(jnp.float32)
    x1, x2 = x32[..., : d // 2], x32[..., d // 2:]
    out = jnp.concatenate([x1 * cos - x2 * sin, x2 * cos + x1 * sin], axis=-1)
    return out.astype(x.dtype)


def moba_attention(q, k, v):
    B, S, H, Dh = q.shape
    nb = -(-S // MOBA_BLOCK)
    n_sel = min(MOBA_TOPK, nb)
    pad = nb * MOBA_BLOCK - S
    kp = jnp.pad(k, ((0, 0), (0, pad), (0, 0), (0, 0)))
    vp = jnp.pad(v, ((0, 0), (0, pad), (0, 0), (0, 0)))
    kb = kp.reshape(B, nb, MOBA_BLOCK, H, Dh).transpose(0, 3, 1, 2, 4)
    vb = vp.reshape(B, nb, MOBA_BLOCK, H, Dh).transpose(0, 3, 1, 2, 4)
    k_mean = jnp.mean(kb.astype(jnp.float32), axis=3)
    scale = Dh ** -0.5
    bidx = jnp.arange(B)[:, None, None, None]
    hidx = jnp.arange(H)[None, None, :, None]

    def chunk(c):
        start = c * MOBA_Q_CHUNK
        qc = lax.dynamic_slice_in_dim(q, start, MOBA_Q_CHUNK, axis=1)
        blk = start // MOBA_BLOCK
        qpos = start + jnp.arange(MOBA_Q_CHUNK)
        gate = jnp.einsum('bqhd,bhnd->bqhn', qc.astype(jnp.float32), k_mean)
        past = jnp.arange(nb) < blk
        gate = jnp.where(past[None, None, None, :], gate, -jnp.inf)
        _, sel = lax.top_k(gate, n_sel)
        sel_valid = jnp.arange(n_sel) < blk
        k_sel = kb[bidx, hidx, sel]
        v_sel = vb[bidx, hidx, sel]
        s_sel = jnp.einsum('bqhd,bqhkjd->bqhkj', qc, k_sel).astype(jnp.float32) * scale
        s_sel = jnp.where(sel_valid[None, None, None, :, None], s_sel, -jnp.inf)
        s_sel = s_sel.reshape(B, MOBA_Q_CHUNK, H, n_sel * MOBA_BLOCK)
        k_own = lax.dynamic_slice_in_dim(kp, blk * MOBA_BLOCK, MOBA_BLOCK, axis=1)
        v_own = lax.dynamic_slice_in_dim(vp, blk * MOBA_BLOCK, MOBA_BLOCK, axis=1)
        s_own = jnp.einsum('bqhd,bjhd->bqhj', qc, k_own).astype(jnp.float32) * scale
        kpos = blk * MOBA_BLOCK + jnp.arange(MOBA_BLOCK)
        causal = kpos[None, :] <= qpos[:, None]
        s_own = jnp.where(causal[None, :, None, :], s_own, -jnp.inf)
        p = jax.nn.softmax(jnp.concatenate([s_sel, s_own], axis=-1), axis=-1).astype(v.dtype)
        p_sel = p[..., : n_sel * MOBA_BLOCK].reshape(B, MOBA_Q_CHUNK, H, n_sel, MOBA_BLOCK)
        p_own = p[..., n_sel * MOBA_BLOCK:]
        return (jnp.einsum('bqhkj,bqhkjd->bqhd', p_sel, v_sel)
                + jnp.einsum('bqhj,bjhd->bqhd', p_own, v_own))

    outs = lax.map(chunk, jnp.arange(S // MOBA_Q_CHUNK))
    return outs.transpose(1, 0, 2, 3, 4).reshape(B, S, H, Dh)


def mla_attention(q_nope, q_rope, k_nope, k_rope, v):
    B, S, H, _ = q_nope.shape
    scale = (MLA_NOPE_DIM + MLA_ROPE_DIM) ** -0.5
    kpos = jnp.arange(S)

    def block(c):
        start = c * MLA_Q_BLOCK
        qn = lax.dynamic_slice_in_dim(q_nope, start, MLA_Q_BLOCK, axis=1)
        qr = lax.dynamic_slice_in_dim(q_rope, start, MLA_Q_BLOCK, axis=1)
        s = (jnp.einsum('bqhd,bkhd->bhqk', qn, k_nope)
             + jnp.einsum('bqhd,bkd->bhqk', qr, k_rope)).astype(jnp.float32) * scale
        qpos = start + jnp.arange(MLA_Q_BLOCK)
        s = jnp.where((kpos[None, :] <= qpos[:, None])[None, None], s, -jnp.inf)
        p = jax.nn.softmax(s, axis=-1).astype(v.dtype)
        return jnp.einsum('bhqk,bkhd->bqhd', p, v)

    outs = lax.map(block, jnp.arange(S // MLA_Q_BLOCK))
    return outs.transpose(1, 0, 2, 3, 4).reshape(B, S, H * MLA_V_DIM)


def setup_inputs(seed: int = 0) -> dict:
    key = jax.random.key(seed)
    ks = jax.random.split(key, 17)

    def dense(k, shape, fan_in):
        return jax.random.normal(k, shape, jnp.float32) * (fan_in ** -0.5)

    def gain(k, shape):
        return 1.0 + 0.02 * jax.random.normal(k, shape, jnp.float32)

    L = DEPTH
    return {
        "x": jax.random.normal(ks[0], (BATCH, SEQ, D_MODEL), jnp.float32),
        "positions": jnp.broadcast_to(jnp.arange(SEQ, dtype=jnp.int32), (BATCH, SEQ)),
        "attn_norm": gain(ks[1], (L, D_MODEL)),
        "w_in": dense(ks[2], (L, D_MODEL, IN_COLS), D_MODEL),
        "q_norm": gain(ks[3], (L, MLA_Q_RANK)),
        "w_uq": dense(ks[4], (L, MLA_Q_RANK, MLA_HEADS * (MLA_NOPE_DIM + MLA_ROPE_DIM)), MLA_Q_RANK),
        "kv_norm": gain(ks[5], (L, MLA_KV_RANK)),
        "w_ukv": dense(ks[6], (L, MLA_KV_RANK, MLA_HEADS * (MLA_NOPE_DIM + MLA_V_DIM)), MLA_KV_RANK),
        "w_branch_a": dense(ks[7], (L, MOBA_WIDTH, D_MODEL), MOBA_WIDTH),
        "w_branch_b": dense(ks[8], (L, MLA_WIDTH, D_MODEL), MLA_WIDTH),
        "w_out": dense(ks[9], (L, D_MODEL, D_MODEL), D_MODEL),
        "ffn_norm": gain(ks[10], (L, D_MODEL)),
        "w_gate": dense(ks[11], (L, D_MODEL, D_FF), D_MODEL),
        "w_up": dense(ks[12], (L, D_MODEL, D_FF), D_MODEL),
        "w_down": dense(ks[13], (L, D_FF, D_MODEL), D_FF),
        "final_norm": gain(ks[14], (D_MODEL,)),
    }


def reference(x, positions, attn_norm, w_in, q_norm, w_uq, kv_norm, w_ukv, w_branch_a, w_branch_b,
              w_out, ffn_norm, w_gate, w_up, w_down, final_norm):
    B, S, _ = x.shape
    split_at = np.cumsum(np.array(IN_SIZES))[:-1].tolist()
    h = x
    for l in range(DEPTH):
        xn = rmsnorm(h, attn_norm[l])
        proj = xn @ w_in[l]
        q_a, k_a, v_a, c_q, c_kv, k_r, gates = jnp.split(proj, split_at, axis=-1)

        q_a = q_a.reshape(B, S, MOBA_HEADS, MOBA_HEAD_DIM)
        k_a = k_a.reshape(B, S, MOBA_HEADS, MOBA_HEAD_DIM)
        v_a = v_a.reshape(B, S, MOBA_HEADS, MOBA_HEAD_DIM)
        q_a = jnp.concatenate([rope(q_a[..., :PARTIAL_ROPE_DIM], positions), q_a[..., PARTIAL_ROPE_DIM:]], axis=-1)
        k_a = jnp.concatenate([rope(k_a[..., :PARTIAL_ROPE_DIM], positions), k_a[..., PARTIAL_ROPE_DIM:]], axis=-1)
        y_a = moba_attention(q_a, k_a, v_a).reshape(B, S, MOBA_WIDTH)

        qh = (rmsnorm(c_q, q_norm[l]) @ w_uq[l]).reshape(B, S, MLA_HEADS, MLA_NOPE_DIM + MLA_ROPE_DIM)
        q_nope, q_rope = qh[..., :MLA_NOPE_DIM], rope(qh[..., MLA_NOPE_DIM:], positions)
        kv = (rmsnorm(c_kv, kv_norm[l]) @ w_ukv[l]).reshape(B, S, MLA_HEADS, MLA_NOPE_DIM + MLA_V_DIM)
        k_nope, v_b = kv[..., :MLA_NOPE_DIM], kv[..., MLA_NOPE_DIM:]
        k_rope = rope(k_r[:, :, None, :], positions)[:, :, 0, :]
        y_b = mla_attention(q_nope, q_rope, k_nope, k_rope, v_b)

        g = jax.nn.sigmoid(gates.astype(jnp.float32)).astype(x.dtype)
        g_a, g_b = g[..., :D_MODEL], g[..., D_MODEL:]
        mixed = g_a * (y_a @ w_branch_a[l]) + g_b * (y_b @ w_branch_b[l])
        h = h + mixed @ w_out[l]

        hn = rmsnorm(h, ffn_norm[l])
        h = h + (jax.nn.silu(hn @ w_gate[l]) * (hn @ w_up[l])) @ w_down[l]
    return rmsnorm(h, final_norm)
```

```python
import functools

import jax
import jax.numpy as jnp
from jax import lax
from jax.experimental import pallas as pl
from jax.experimental.pallas import tpu as pltpu

F32 = jnp.float32
BF16 = jnp.bfloat16

D_MODEL = 2048
SEQ = 16384
HEADS = 8
HEAD_DIM = 128
WIDTH = HEADS * HEAD_DIM
MOBA_BLOCK = 256
MOBA_TOPK = 3
N_BLOCKS = SEQ // MOBA_BLOCK
MOBA_ROPE_HALF = 16
MLA_ROPE_DIM = 64
MLA_ROPE_HALF = 32
MLA_Q_RANK = 512
MLA_KV_RANK = 256
ROPE_THETA = 500000.0
D_FF = 5632
EPS = 1e-6
NEG = -1e30
LANES = 128

MOBA_SCALE = HEAD_DIM ** -0.5
MLA_SCALE = (HEAD_DIM + MLA_ROPE_DIM) ** -0.5

VMEM_LIMIT = 52 * 1024 * 1024


def _params(*sem):
    return pltpu.CompilerParams(dimension_semantics=sem, vmem_limit_bytes=VMEM_LIMIT)


def _rmsnorm_bf16(x, g):
    ms = jnp.mean(x * x, axis=-1, keepdims=True)
    return ((x * lax.rsqrt(ms + EPS)) * g).astype(BF16)


def _rope(x, cos, sin, half):
    lane = lax.broadcasted_iota(jnp.int32, x.shape, 1)
    partner = jnp.where(lane < half, pltpu.roll(x, LANES - half, 1), pltpu.roll(x, half, 1))
    return x * cos + partner * sin


def _rope_table_kernel(pos_ref, f_ref, ca_ref, sa_ref, cb_ref, sb_ref):
    ang = pos_ref[...].astype(F32) * f_ref[...]
    c = jnp.cos(ang)
    s = jnp.sin(ang)
    lane = lax.broadcasted_iota(jnp.int32, c.shape, 1)
    ha, hb = MOBA_ROPE_HALF, MLA_ROPE_HALF
    ca_ref[...] = jnp.where(lane < 2 * ha, c, 1.0)
    sa_ref[...] = jnp.where(lane < ha, -s, jnp.where(lane < 2 * ha, s, 0.0))
    c2 = pltpu.roll(c, LANES - 2 * ha, 1)
    s2 = pltpu.roll(s, LANES - 2 * ha, 1)
    cb_ref[...] = jnp.where(lane < 2 * hb, c2, 1.0)
    sb_ref[...] = jnp.where(lane < hb, -s2, jnp.where(lane < 2 * hb, s2, 0.0))


def _rope_tables(pos, freqs, tm=2048):
    out = jax.ShapeDtypeStruct((SEQ, LANES), F32)
    row = pl.BlockSpec((tm, LANES), lambda i: (i, 0))
    return pl.pallas_call(
        _rope_table_kernel,
        grid=(SEQ // tm,),
        in_specs=[pl.BlockSpec((tm, 1), lambda i: (i, 0)),
                  pl.BlockSpec((1, LANES), lambda i: (0, 0))],
        out_specs=[row, row, row, row],
        out_shape=[out, out, out, out],
        compiler_params=_params("parallel"),
        name="rope_tables",
    )(pos, freqs)


def _qkv_kernel(x_ref, g_ref, w_ref, cos_ref, sin_ref, o_ref, kmean_ref, xn_ref):
    j = pl.program_id(1)

    @pl.when(j == 0)
    def _():
        xn_ref[...] = _rmsnorm_bf16(x_ref[...], g_ref[...])

    acc = jnp.dot(xn_ref[...], w_ref[...], preferred_element_type=F32)
    tm = acc.shape[0]

    @pl.when(j == 0)
    def _():
        cos, sin = cos_ref[...], sin_ref[...]
        for h in range(HEADS):
            sl = slice(h * HEAD_DIM, (h + 1) * HEAD_DIM)
            o_ref[:, sl] = (_rope(acc[:, sl], cos, sin, MOBA_ROPE_HALF) * MOBA_SCALE).astype(BF16)

    @pl.when(j == 1)
    def _():
        cos, sin = cos_ref[...], sin_ref[...]
        for h in range(HEADS):
            sl = slice(h * HEAD_DIM, (h + 1) * HEAD_DIM)
            r = _rope(acc[:, sl], cos, sin, MOBA_ROPE_HALF)
            o_ref[:, sl] = r.astype(BF16)
            for b in range(tm // MOBA_BLOCK):
                rows = r[b * MOBA_BLOCK:(b + 1) * MOBA_BLOCK]
                kmean_ref[0, b:b + 1, sl] = jnp.mean(rows, axis=0, keepdims=True)

    @pl.when(j == 2)
    def _():
        o_ref[...] = acc.astype(BF16)


def _qkv_proj(x, g, w_qkv, cos_a, sin_a, tm=512):
    nb = tm // MOBA_BLOCK
    return pl.pallas_call(
        _qkv_kernel,
        grid=(SEQ // tm, 3),
        in_specs=[pl.BlockSpec((tm, D_MODEL), lambda i, j: (i, 0)),
                  pl.BlockSpec((1, D_MODEL), lambda i, j: (0, 0)),
                  pl.BlockSpec((D_MODEL, WIDTH), lambda i, j: (0, j)),
                  pl.BlockSpec((tm, LANES), lambda i, j: (i, 0)),
                  pl.BlockSpec((tm, LANES), lambda i, j: (i, 0))],
        out_specs=[pl.BlockSpec((tm, WIDTH), lambda i, j: (i, j)),
                   pl.BlockSpec((1, nb, WIDTH), lambda i, j: (i, 0, 0))],
        out_shape=[jax.ShapeDtypeStruct((SEQ, 3 * WIDTH), BF16),
                   jax.ShapeDtypeStruct((SEQ // tm, nb, WIDTH), F32)],
        scratch_shapes=[pltpu.VMEM((tm, D_MODEL), BF16)],
        compiler_params=_params("parallel", "arbitrary"),
        name="qkv_proj",
    )(x, g, w_qkv, cos_a, sin_a)


def _mla_kernel(x_ref, g_ref, wc_ref, qg_ref, wuq_ref, kvg_ref, wukv_ref, cos_ref, sin_ref,
                qb_ref, kvb_ref, kr_ref):
    xn = _rmsnorm_bf16(x_ref[...], g_ref[...])
    c = jnp.dot(xn, wc_ref[...], preferred_element_type=F32)
    cos, sin = cos_ref[...], sin_ref[...]
    kr = c[:, MLA_Q_RANK + MLA_KV_RANK:]
    kr_ref[...] = _rope(kr, cos, sin, MLA_ROPE_HALF).astype(BF16)

    cq = _rmsnorm_bf16(c[:, :MLA_Q_RANK], qg_ref[...])
    q = jnp.dot(cq, wuq_ref[...], preferred_element_type=F32)
    qb_ref[:, :WIDTH] = (q[:, :WIDTH] * MLA_SCALE).astype(BF16)
    for h in range(HEADS):
        sl = slice(WIDTH + h * HEAD_DIM, WIDTH + (h + 1) * HEAD_DIM)
        qb_ref[:, sl] = (_rope(q[:, sl], cos, sin, MLA_ROPE_HALF) * MLA_SCALE).astype(BF16)

    ckv = _rmsnorm_bf16(c[:, MLA_Q_RANK:MLA_Q_RANK + MLA_KV_RANK], kvg_ref[...])
    kvb_ref[...] = jnp.dot(ckv, wukv_ref[...], preferred_element_type=F32).astype(BF16)


def _mla_proj(x, g, w_c, qg, w_uq, kvg, w_ukv, cos_b, sin_b, tm=512):
    nc = w_c.shape[1]
    full = lambda shape: pl.BlockSpec(shape, lambda i: (0, 0))
    return pl.pallas_call(
        _mla_kernel,
        grid=(SEQ // tm,),
        in_specs=[pl.BlockSpec((tm, D_MODEL), lambda i: (i, 0)),
                  full((1, D_MODEL)), full((D_MODEL, nc)),
                  full((1, MLA_Q_RANK)), full((MLA_Q_RANK, 2 * WIDTH)),
                  full((1, MLA_KV_RANK)), full((MLA_KV_RANK, 2 * WIDTH)),
                  pl.BlockSpec((tm, LANES), lambda i: (i, 0)),
                  pl.BlockSpec((tm, LANES), lambda i: (i, 0))],
        out_specs=[pl.BlockSpec((tm, 2 * WIDTH), lambda i: (i, 0)),
                   pl.BlockSpec((tm, 2 * WIDTH), lambda i: (i, 0)),
                   pl.BlockSpec((tm, LANES), lambda i: (i, 0))],
        out_shape=[jax.ShapeDtypeStruct((SEQ, 2 * WIDTH), BF16),
                   jax.ShapeDtypeStruct((SEQ, 2 * WIDTH), BF16),
                   jax.ShapeDtypeStruct((SEQ, LANES), BF16)],
        compiler_params=_params("parallel"),
        name="mla_proj",
    )(x, g, w_c, qg, w_uq, kvg, w_ukv, cos_b, sin_b)


def _select_kernel(q_ref, kmean_ref, bias_ref):
    i = pl.program_id(1)
    q = q_ref[...].astype(F32)
    gate = lax.dot_general(q, kmean_ref[...], (((1,), (1,)), ((), ())),
                           preferred_element_type=F32,
                           precision=lax.Precision.HIGHEST)
    col = lax.broadcasted_iota(jnp.int32, gate.shape, 1)
    past = col < i
    g = jnp.where(past, gate, -jnp.inf)
    sel = col == i
    for _ in range(MOBA_TOPK):
        mx = jnp.max(g, axis=1, keepdims=True)
        cand = jnp.where((g == mx) & past, col, LANES)
        idx = jnp.min(cand, axis=1, keepdims=True)
        hit = col == idx
        sel = sel | hit
        g = jnp.where(hit, -jnp.inf, g)
    bias_ref[...] = jnp.where(sel, 0.0, jnp.where(col < N_BLOCKS, NEG, 0.0)).astype(BF16)


def _moba_select(qkv, kmean_pad):
    return pl.pallas_call(
        _select_kernel,
        grid=(HEADS, N_BLOCKS),
        in_specs=[pl.BlockSpec((MOBA_BLOCK, HEAD_DIM), lambda h, i: (i, h)),
                  pl.BlockSpec((LANES, HEAD_DIM), lambda h, i: (0, h))],
        out_specs=pl.BlockSpec((MOBA_BLOCK, LANES), lambda h, i: (i, h)),
        out_shape=jax.ShapeDtypeStruct((SEQ, WIDTH), BF16),
        compiler_params=_params("parallel", "parallel"),
        name="moba_select",
    )(qkv, kmean_pad)


def _flash_kernel(q_ref, qx_ref, k_ref, kx_ref, v_ref, o_ref, *, tq):
    i = pl.program_id(1)
    qa = jnp.concatenate([q_ref[...], qx_ref[...]], axis=1)

    def scores(j):
        rows = pl.ds(pl.multiple_of(j * tq, tq), tq)
        ka = jnp.concatenate([k_ref[rows, :], kx_ref[rows, :]], axis=1)
        s = lax.dot_general(qa, ka, (((1,), (1,)), ((), ())), preferred_element_type=F32)
        return s, v_ref[rows, :]

    s, v = scores(i)
    row = lax.broadcasted_iota(jnp.int32, s.shape, 0)
    col = lax.broadcasted_iota(jnp.int32, s.shape, 1)
    s = jnp.where(col <= row, s, NEG)
    m = jnp.max(s, axis=1, keepdims=True)
    p = jnp.exp(s - m)
    l = jnp.sum(p, axis=1, keepdims=True)
    acc = jnp.dot(p.astype(BF16), v, preferred_element_type=F32)

    def body(j, carry):
        m, l, acc = carry
        s, v = scores(j)
        m_new = jnp.maximum(m, jnp.max(s, axis=1, keepdims=True))
        alpha = jnp.exp(m - m_new)
        p = jnp.exp(s - m_new)
        l = alpha * l + jnp.sum(p, axis=1, keepdims=True)
        acc = alpha * acc + jnp.dot(p.astype(BF16), v, preferred_element_type=F32)
        return m_new, l, acc

    m, l, acc = lax.fori_loop(0, i, body, (m, l, acc))
    o_ref[...] = (acc / l).astype(o_ref.dtype)


def _flash(q_arr, q_off, qx_arr, qx_off, k_arr, k_off, kx_arr, kx_off, kx_per_head,
           v_arr, v_off, name, tq=256):
    kx_map = (lambda h, i: (0, kx_off + h)) if kx_per_head else (lambda h, i: (0, kx_off))
    return pl.pallas_call(
        functools.partial(_flash_kernel, tq=tq),
        grid=(HEADS, SEQ // tq),
        in_specs=[pl.BlockSpec((tq, HEAD_DIM), lambda h, i: (i, q_off + h)),
                  pl.BlockSpec((tq, LANES), lambda h, i: (i, qx_off + h)),
                  pl.BlockSpec((SEQ, HEAD_DIM), lambda h, i: (0, k_off + h)),
                  pl.BlockSpec((SEQ, LANES), kx_map),
                  pl.BlockSpec((SEQ, HEAD_DIM), lambda h, i: (0, v_off + h))],
        out_specs=pl.BlockSpec((tq, HEAD_DIM), lambda h, i: (i, h)),
        out_shape=jax.ShapeDtypeStruct((SEQ, WIDTH), BF16),
        compiler_params=_params("parallel", "arbitrary"),
        name=name,
    )(q_arr, qx_arr, k_arr, kx_arr, v_arr)


def _gates_kernel(x_ref, g_ref, w_ref, o_ref, xn_ref):
    @pl.when(pl.program_id(1) == 0)
    def _():
        xn_ref[...] = _rmsnorm_bf16(x_ref[...], g_ref[...])

    z = jnp.dot(xn_ref[...], w_ref[...], preferred_element_type=F32)
    o_ref[...] = jax.nn.sigmoid(z)


def _gates(x, g, w_g, tm=512, tn=1024):
    n = w_g.shape[1]
    return pl.pallas_call(
        _gates_kernel,
        grid=(SEQ // tm, n // tn),
        in_specs=[pl.BlockSpec((tm, D_MODEL), lambda i, j: (i, 0)),
                  pl.BlockSpec((1, D_MODEL), lambda i, j: (0, 0)),
                  pl.BlockSpec((D_MODEL, tn), lambda i, j: (0, j))],
        out_specs=pl.BlockSpec((tm, tn), lambda i, j: (i, j)),
        out_shape=jax.ShapeDtypeStruct((SEQ, n), F32),
        scratch_shapes=[pltpu.VMEM((tm, D_MODEL), BF16)],
        compiler_params=_params("parallel", "arbitrary"),
        name="gates",
    )(x, g, w_g)


def _mix1_kernel(ya_ref, yb_ref, wa_ref, wb_ref, ga_ref, gb_ref, o_ref):
    a = jnp.dot(ya_ref[...], wa_ref[...], preferred_element_type=F32)
    b = jnp.dot(yb_ref[...], wb_ref[...], preferred_element_type=F32)
    o_ref[...] = (ga_ref[...] * a + gb_ref[...] * b).astype(BF16)


def _mix1(ya, yb, wa, wb, gates, tm=512, tn=1024):
    nj = D_MODEL // tn
    return pl.pallas_call(
        _mix1_kernel,
        grid=(SEQ // tm, nj),
        in_specs=[pl.BlockSpec((tm, WIDTH), lambda i, j: (i, 0)),
                  pl.BlockSpec((tm, WIDTH), lambda i, j: (i, 0)),
                  pl.BlockSpec((WIDTH, tn), lambda i, j: (0, j)),
                  pl.BlockSpec((WIDTH, tn), lambda i, j: (0, j)),
                  pl.BlockSpec((tm, tn), lambda i, j: (i, j)),
                  pl.BlockSpec((tm, tn), lambda i, j: (i, j + nj))],
        out_specs=pl.BlockSpec((tm, tn), lambda i, j: (i, j)),
        out_shape=jax.ShapeDtypeStruct((SEQ, D_MODEL), BF16),
        compiler_params=_params("parallel", "arbitrary"),
        name="mix1",
    )(ya, yb, wa, wb, gates, gates)


def _mix2_kernel(m_ref, w_ref, x_ref, o_ref):
    o_ref[...] = x_ref[...] + jnp.dot(m_ref[...], w_ref[...], preferred_element_type=F32)


def _mix2(mixed, w_out, x, tm=512, tn=1024):
    return pl.pallas_call(
        _mix2_kernel,
        grid=(SEQ // tm, D_MODEL // tn),
        in_specs=[pl.BlockSpec((tm, D_MODEL), lambda i, j: (i, 0)),
                  pl.BlockSpec((D_MODEL, tn), lambda i, j: (0, j)),
                  pl.BlockSpec((tm, tn), lambda i, j: (i, j))],
        out_specs=pl.BlockSpec((tm, tn), lambda i, j: (i, j)),
        out_shape=jax.ShapeDtypeStruct((SEQ, D_MODEL), F32),
        compiler_params=_params("parallel", "arbitrary"),
        name="mix2",
    )(mixed, w_out, x)


def _ffn_kernel(h_ref, g_ref, wg_ref, wu_ref, wd_ref, fg_ref, o_ref, hn_ref, acc_ref):
    f = pl.program_id(1)

    @pl.when(f == 0)
    def _():
        hn_ref[...] = _rmsnorm_bf16(h_ref[...], g_ref[...])
        acc_ref[...] = jnp.zeros_like(acc_ref)

    hn = hn_ref[...]
    gate = jnp.dot(hn, wg_ref[...], preferred_element_type=F32)
    up = jnp.dot(hn, wu_ref[...], preferred_element_type=F32)
    act = (jax.nn.silu(gate) * up).astype(BF16)
    acc_ref[...] += jnp.dot(act, wd_ref[...], preferred_element_type=F32)

    @pl.when(f == pl.num_programs(1) - 1)
    def _():
        y = h_ref[...] + acc_ref[...]
        ms = jnp.mean(y * y, axis=-1, keepdims=True)
        o_ref[...] = (y * lax.rsqrt(ms + EPS)) * fg_ref[...]


def _ffn(h, g, wg, wu, wd, fg, tm=512, tf=512):
    return pl.pallas_call(
        _ffn_kernel,
        grid=(SEQ // tm, D_FF // tf),
        in_specs=[pl.BlockSpec((tm, D_MODEL), lambda i, f: (i, 0)),
                  pl.BlockSpec((1, D_MODEL), lambda i, f: (0, 0)),
                  pl.BlockSpec((D_MODEL, tf), lambda i, f: (0, f)),
                  pl.BlockSpec((D_MODEL, tf), lambda i, f: (0, f)),
                  pl.BlockSpec((tf, D_MODEL), lambda i, f: (f, 0)),
                  pl.BlockSpec((1, D_MODEL), lambda i, f: (0, 0))],
        out_specs=pl.BlockSpec((tm, D_MODEL), lambda i, f: (i, 0)),
        out_shape=jax.ShapeDtypeStruct((SEQ, D_MODEL), F32),
        scratch_shapes=[pltpu.VMEM((tm, D_MODEL), BF16),
                        pltpu.VMEM((tm, D_MODEL), F32)],
        compiler_params=_params("parallel", "arbitrary"),
        name="ffn",
    )(h, g, wg, wu, wd, fg)


def _rope_freqs():
    def inv_freq(d):
        return ROPE_THETA ** (-jnp.arange(0, d, 2, dtype=F32) / d)

    fa = inv_freq(2 * MOBA_ROPE_HALF)
    fb = inv_freq(2 * MLA_ROPE_HALF)
    pad = jnp.zeros((LANES - 2 * MOBA_ROPE_HALF - 2 * MLA_ROPE_HALF,), F32)
    return jnp.concatenate([fa, fa, fb, fb, pad]).reshape(1, LANES)


def kernel(x, positions, attn_norm, w_in, q_norm, w_uq, kv_norm, w_ukv, w_branch_a, w_branch_b,
           w_out, ffn_norm, w_gate, w_up, w_down, final_norm):
    assert x.shape == (1, SEQ, D_MODEL) and w_in.shape[0] == 1
    x2 = x.reshape(SEQ, D_MODEL)
    pos = positions.reshape(SEQ, 1)
    row = lambda v: v.reshape(1, -1).astype(F32)

    w = w_in[0]
    c0 = 3 * WIDTH
    c1 = c0 + MLA_Q_RANK + MLA_KV_RANK + MLA_ROPE_DIM
    w_qkv = w[:, :c0].astype(BF16)
    w_c = jnp.pad(w[:, c0:c1], ((0, 0), (0, LANES - MLA_ROPE_DIM))).astype(BF16)
    w_g = w[:, c1:].astype(BF16)
    uq = w_uq[0].reshape(MLA_Q_RANK, HEADS, HEAD_DIM + MLA_ROPE_DIM)
    uq_rope = jnp.pad(uq[:, :, HEAD_DIM:], ((0, 0), (0, 0), (0, LANES - MLA_ROPE_DIM)))
    w_uq2 = jnp.concatenate([uq[:, :, :HEAD_DIM].reshape(MLA_Q_RANK, WIDTH),
                             uq_rope.reshape(MLA_Q_RANK, WIDTH)], axis=1).astype(BF16)
    ukv = w_ukv[0].reshape(MLA_KV_RANK, HEADS, 2 * HEAD_DIM)
    w_ukv2 = jnp.concatenate([ukv[:, :, :HEAD_DIM].reshape(MLA_KV_RANK, WIDTH),
                              ukv[:, :, HEAD_DIM:].reshape(MLA_KV_RANK, WIDTH)], axis=1).astype(BF16)

    cos_a, sin_a, cos_b, sin_b = _rope_tables(pos, _rope_freqs())

    g_attn = row(attn_norm[0])
    qkv, kmean = _qkv_proj(x2, g_attn, w_qkv, cos_a, sin_a)
    qb, kvb, krx = _mla_proj(x2, g_attn, w_c, row(q_norm[0]), w_uq2, row(kv_norm[0]), w_ukv2,
                             cos_b, sin_b)

    kmean_pad = jnp.pad(kmean.reshape(N_BLOCKS, WIDTH), ((0, LANES - N_BLOCKS), (0, 0)))
    bias = _moba_select(qkv, kmean_pad)
    blk = jnp.arange(SEQ, dtype=jnp.int32)[:, None] // MOBA_BLOCK
    onehot = (blk == jnp.arange(LANES, dtype=jnp.int32)[None, :]).astype(BF16)

    nh = HEADS
    y_a = _flash(qkv, 0, bias, 0, qkv, nh, onehot, 0, False, qkv, 2 * nh, "flash_moba")
    y_b = _flash(qb, 0, qb, nh, kvb, 0, krx, 0, False, kvb, nh, "flash_mla")

    gates = _gates(x2, g_attn, w_g)
    mixed = _mix1(y_a, y_b, w_branch_a[0].astype(BF16), w_branch_b[0].astype(BF16), gates)
    h = _mix2(mixed, w_out[0].astype(BF16), x2)
    out = _ffn(h, row(ffn_norm[0]), w_gate[0].astype(BF16), w_up[0].astype(BF16),
               w_down[0].astype(BF16), row(final_norm))
    return out.reshape(1, SEQ, D_MODEL)
```

```python
import functools

import jax
import jax.numpy as jnp
from jax import lax
from jax.experimental import pallas as pl
from jax.experimental.pallas import tpu as pltpu

F32 = jnp.float32
BF16 = jnp.bfloat16

D_MODEL = 2048
SEQ = 16384
HEADS = 8
HEAD_DIM = 128
WIDTH = HEADS * HEAD_DIM
MOBA_BLOCK = 256
MOBA_TOPK = 3
N_BLOCKS = SEQ // MOBA_BLOCK
MOBA_ROPE_HALF = 16
MLA_ROPE_DIM = 64
MLA_ROPE_HALF = 32
MLA_Q_RANK = 512
MLA_KV_RANK = 256
ROPE_THETA = 500000.0
D_FF = 5632
EPS = 1e-6
NEG = -1e30
LANES = 128
SUBLANES = 8
FLASH_TQ = 1024
FLASH_TK = 512
FLASH_CHUNK = 64

LOG2E = 1.4426950408889634
MOBA_SCALE = HEAD_DIM ** -0.5 * LOG2E
MLA_SCALE = (HEAD_DIM + MLA_ROPE_DIM) ** -0.5 * LOG2E

VMEM_LIMIT = 52 * 1024 * 1024


def _params(*sem):
    return pltpu.CompilerParams(dimension_semantics=sem, vmem_limit_bytes=VMEM_LIMIT)


def _rmsnorm_bf16(x, g):
    ms = jnp.mean(x * x, axis=-1, keepdims=True)
    return ((x * lax.rsqrt(ms + EPS)) * g).astype(BF16)


def _rope(x, cos, sin, half):
    lane = lax.broadcasted_iota(jnp.int32, x.shape, 1)
    partner = jnp.where(lane < half, pltpu.roll(x, LANES - half, 1), pltpu.roll(x, half, 1))
    return x * cos + partner * sin


def _rope_table_kernel(pos_ref, f_ref, ca_ref, sa_ref, cb_ref, sb_ref):
    ang = pos_ref[...].astype(F32) * f_ref[...]
    c = jnp.cos(ang)
    s = jnp.sin(ang)
    lane = lax.broadcasted_iota(jnp.int32, c.shape, 1)
    ha, hb = MOBA_ROPE_HALF, MLA_ROPE_HALF
    ca_ref[...] = jnp.where(lane < 2 * ha, c, 1.0)
    sa_ref[...] = jnp.where(lane < ha, -s, jnp.where(lane < 2 * ha, s, 0.0))
    c2 = pltpu.roll(c, LANES - 2 * ha, 1)
    s2 = pltpu.roll(s, LANES - 2 * ha, 1)
    cb_ref[...] = jnp.where(lane < 2 * hb, c2, 1.0)
    sb_ref[...] = jnp.where(lane < hb, -s2, jnp.where(lane < 2 * hb, s2, 0.0))


def _rope_tables(pos, freqs, tm=2048):
    out = jax.ShapeDtypeStruct((SEQ, LANES), F32)
    row = pl.BlockSpec((tm, LANES), lambda i: (i, 0))
    return pl.pallas_call(
        _rope_table_kernel,
        grid=(SEQ // tm,),
        in_specs=[pl.BlockSpec((tm, 1), lambda i: (i, 0)),
                  pl.BlockSpec((1, LANES), lambda i: (0, 0))],
        out_specs=[row, row, row, row],
        out_shape=[out, out, out, out],
        compiler_params=_params("parallel"),
        name="rope_tables",
    )(pos, freqs)


_NT = (((1,), (1,)), ((), ()))


def _store_vt(vt_ref, vt):
    for h in range(HEADS):
        vt_ref[h, 0] = vt[h * HEAD_DIM:(h + 1) * HEAD_DIM, :].astype(BF16)


def _qkv_kernel(x_ref, g_ref, w_ref, wvt_ref, cos_ref, sin_ref, o_ref, vt_ref, kmean_ref, xn_ref):
    j = pl.program_id(1)
    tm = x_ref.shape[0]

    @pl.when(j == 0)
    def _():
        xn_ref[...] = _rmsnorm_bf16(x_ref[...], g_ref[...])

    def roped_heads():
        acc = jnp.dot(xn_ref[...], w_ref[...], preferred_element_type=F32)
        cos, sin = cos_ref[...], sin_ref[...]
        for h in range(HEADS):
            sl = slice(h * HEAD_DIM, (h + 1) * HEAD_DIM)
            yield sl, _rope(acc[:, sl], cos, sin, MOBA_ROPE_HALF)

    @pl.when(j == 0)
    def _():
        for sl, r in roped_heads():
            o_ref[:, sl] = (r * MOBA_SCALE).astype(BF16)

    @pl.when(j == 1)
    def _():
        for sl, r in roped_heads():
            o_ref[:, sl] = r.astype(BF16)
            for b in range(tm // MOBA_BLOCK):
                rows = r[b * MOBA_BLOCK:(b + 1) * MOBA_BLOCK]
                kmean_ref[0, b:b + 1, sl] = jnp.mean(rows, axis=0, keepdims=True)

    @pl.when(j == 2)
    def _():
        vt = lax.dot_general(wvt_ref[...], xn_ref[...], _NT, preferred_element_type=F32)
        _store_vt(vt_ref, vt)


def _qkv_proj(x, g, w_qk, w_vt, cos_a, sin_a, tm):
    nb = tm // MOBA_BLOCK
    return pl.pallas_call(
        _qkv_kernel,
        grid=(SEQ // tm, 3),
        in_specs=[pl.BlockSpec((tm, D_MODEL), lambda i, j: (i, 0)),
                  pl.BlockSpec((1, D_MODEL), lambda i, j: (0, 0)),
                  pl.BlockSpec((D_MODEL, WIDTH), lambda i, j: (0, jnp.minimum(j, 1))),
                  pl.BlockSpec((WIDTH, D_MODEL), lambda i, j: (0, 0)),
                  pl.BlockSpec((tm, LANES), lambda i, j: (i, 0)),
                  pl.BlockSpec((tm, LANES), lambda i, j: (i, 0))],
        out_specs=[pl.BlockSpec((tm, WIDTH), lambda i, j: (i, jnp.minimum(j, 1))),
                   pl.BlockSpec((HEADS, 1, HEAD_DIM, tm), lambda i, j: (0, i, 0, 0)),
                   pl.BlockSpec((1, nb, WIDTH), lambda i, j: (i, 0, 0))],
        out_shape=[jax.ShapeDtypeStruct((SEQ, 2 * WIDTH), BF16),
                   jax.ShapeDtypeStruct((HEADS, SEQ // tm, HEAD_DIM, tm), BF16),
                   jax.ShapeDtypeStruct((SEQ // tm, nb, WIDTH), F32)],
        scratch_shapes=[pltpu.VMEM((tm, D_MODEL), BF16)],
        compiler_params=_params("parallel", "arbitrary"),
        name="qkv_proj",
    )(x, g, w_qk, w_vt, cos_a, sin_a)


def _mla_kernel(x_ref, g_ref, wc_ref, qg_ref, wuq_ref, kvg_ref, wuk_ref, wuvt_ref, cos_ref, sin_ref,
                qb_ref, kn_ref, vt_ref, kr_ref):
    xn = _rmsnorm_bf16(x_ref[...], g_ref[...])
    c = jnp.dot(xn, wc_ref[...], preferred_element_type=F32)
    cos, sin = cos_ref[...], sin_ref[...]
    kr = c[:, MLA_Q_RANK + MLA_KV_RANK:]
    kr_ref[...] = _rope(kr, cos, sin, MLA_ROPE_HALF).astype(BF16)

    cq = _rmsnorm_bf16(c[:, :MLA_Q_RANK], qg_ref[...])
    q = jnp.dot(cq, wuq_ref[...], preferred_element_type=F32)
    qb_ref[:, :WIDTH] = (q[:, :WIDTH] * MLA_SCALE).astype(BF16)
    for h in range(HEADS):
        sl = slice(WIDTH + h * HEAD_DIM, WIDTH + (h + 1) * HEAD_DIM)
        qb_ref[:, sl] = (_rope(q[:, sl], cos, sin, MLA_ROPE_HALF) * MLA_SCALE).astype(BF16)

    ckv = _rmsnorm_bf16(c[:, MLA_Q_RANK:MLA_Q_RANK + MLA_KV_RANK], kvg_ref[...])
    kn_ref[...] = jnp.dot(ckv, wuk_ref[...], preferred_element_type=F32).astype(BF16)
    _store_vt(vt_ref, lax.dot_general(wuvt_ref[...], ckv, _NT, preferred_element_type=F32))


def _mla_proj(x, g, w_c, qg, w_uq, kvg, w_uk, w_uvt, cos_b, sin_b, tm):
    nc = w_c.shape[1]
    full = lambda shape: pl.BlockSpec(shape, lambda i: (0, 0))
    return pl.pallas_call(
        _mla_kernel,
        grid=(SEQ // tm,),
        in_specs=[pl.BlockSpec((tm, D_MODEL), lambda i: (i, 0)),
                  full((1, D_MODEL)), full((D_MODEL, nc)),
                  full((1, MLA_Q_RANK)), full((MLA_Q_RANK, 2 * WIDTH)),
                  full((1, MLA_KV_RANK)), full((MLA_KV_RANK, WIDTH)), full((WIDTH, MLA_KV_RANK)),
                  pl.BlockSpec((tm, LANES), lambda i: (i, 0)),
                  pl.BlockSpec((tm, LANES), lambda i: (i, 0))],
        out_specs=[pl.BlockSpec((tm, 2 * WIDTH), lambda i: (i, 0)),
                   pl.BlockSpec((tm, WIDTH), lambda i: (i, 0)),
                   pl.BlockSpec((HEADS, 1, HEAD_DIM, tm), lambda i: (0, i, 0, 0)),
                   pl.BlockSpec((tm, LANES), lambda i: (i, 0))],
        out_shape=[jax.ShapeDtypeStruct((SEQ, 2 * WIDTH), BF16),
                   jax.ShapeDtypeStruct((SEQ, WIDTH), BF16),
                   jax.ShapeDtypeStruct((HEADS, SEQ // tm, HEAD_DIM, tm), BF16),
                   jax.ShapeDtypeStruct((SEQ, LANES), BF16)],
        compiler_params=_params("parallel"),
        name="mla_proj",
    )(x, g, w_c, qg, w_uq, kvg, w_uk, w_uvt, cos_b, sin_b)


def _select_kernel(q_ref, kmean_ref, bias_ref):
    q = q_ref[...].astype(F32)
    gate = lax.dot_general(q, kmean_ref[...], (((1,), (1,)), ((), ())),
                           preferred_element_type=F32,
                           precision=lax.Precision.HIGHEST)
    col = lax.broadcasted_iota(jnp.int32, gate.shape, 1)
    row = lax.broadcasted_iota(jnp.int32, gate.shape, 0)
    own = (pl.program_id(1) * gate.shape[0] + row) // MOBA_BLOCK
    past = col < own
    g = jnp.where(past, gate, -jnp.inf)
    sel = col == own
    for _ in range(MOBA_TOPK):
        mx = jnp.max(g, axis=1, keepdims=True)
        cand = jnp.where((g == mx) & past, col, LANES)
        idx = jnp.min(cand, axis=1, keepdims=True)
        hit = col == idx
        sel = sel | hit
        g = jnp.where(hit, -jnp.inf, g)
    bias_ref[...] = jnp.where(sel, 0.0, jnp.where(col < N_BLOCKS, NEG, 0.0)).astype(BF16)


def _moba_select(qkv, kmean_pad, ts=1024):
    return pl.pallas_call(
        _select_kernel,
        grid=(HEADS, SEQ // ts),
        in_specs=[pl.BlockSpec((ts, HEAD_DIM), lambda h, i: (i, h)),
                  pl.BlockSpec((LANES, HEAD_DIM), lambda h, i: (0, h))],
        out_specs=pl.BlockSpec((ts, LANES), lambda h, i: (i, h)),
        out_shape=jax.ShapeDtypeStruct((SEQ, WIDTH), BF16),
        compiler_params=_params("parallel", "parallel"),
        name="moba_select",
    )(qkv, kmean_pad)


def _flash_kernel(q_ref, qx_ref, k_ref, kx_ref, vt_ref, o_ref, s_ref, p_ref, acc_ref, *, tq, tk):
    i = pl.program_id(1)
    diag = tq // tk
    n_tiles = (i + 1) * diag
    qa = jnp.concatenate([q_ref[...], qx_ref[...]], axis=1)

    def tile_of(n):
        n = jnp.minimum(n, n_tiles - 1)
        return jnp.where(n < diag, i * diag + n, n - diag)

    def scores(n):
        rows = pl.ds(pl.multiple_of(tile_of(n) * tk, tk), tk)
        ka = jnp.concatenate([k_ref[rows, :], kx_ref[rows, :]], axis=1)
        return lax.dot_general(ka, qa, _NT, preferred_element_type=F32)

    def diag_scores(n):
        s = scores(n)
        key = lax.broadcasted_iota(jnp.int32, s.shape, 0) + n * tk
        qry = lax.broadcasted_iota(jnp.int32, s.shape, 1)
        return jnp.where(key <= qry, s, NEG)

    chunks = [slice(c * FLASH_CHUNK, (c + 1) * FLASH_CHUNK) for c in range(tk // FLASH_CHUNK)]

    def fold8(x, op):
        out = x[0:SUBLANES]
        for r in range(1, x.shape[0] // SUBLANES):
            out = op(out, x[r * SUBLANES:(r + 1) * SUBLANES])
        return out

    def col_max(s):
        mx = fold8(s[chunks[0], :], jnp.maximum)
        for c in chunks[1:]:
            mx = jnp.maximum(mx, fold8(s[c, :], jnp.maximum))
        return jnp.max(mx, axis=0, keepdims=True)

    def exp_stage(s, p, tile_max, m):
        m_new = jnp.maximum(m, tile_max)
        psum = jnp.zeros((SUBLANES, tq), F32)
        for c in chunks:
            pc = jnp.exp2(s[c, :] - m_new)
            p[c, :] = pc.astype(BF16)
            psum = psum + fold8(pc, jnp.add)
        return jnp.exp2(m - m_new), jnp.sum(psum, axis=0, keepdims=True), m_new

    def value_stage(n, p, alpha, psum, l):
        pv = jnp.dot(vt_ref[0, tile_of(n)], p[...], preferred_element_type=F32)
        acc_ref[...] = alpha * acc_ref[...] + pv
        return alpha * l + psum

    assert diag in (1, 2)
    s_ref[0] = diag_scores(0)
    s_ref[1] = diag_scores(1) if diag == 2 else scores(1)
    acc_ref[...] = jnp.zeros_like(acc_ref)
    carry = exp_stage(s_ref.at[0], p_ref.at[0], col_max(s_ref.at[0]),
                      jnp.full((1, tq), NEG, F32)) + (jnp.zeros((1, tq), F32),)

    def step(k, a, carry):
        alpha0, psum0, m, l = carry
        b = 1 - a
        tile_max = col_max(s_ref.at[b])
        s_ref[a] = scores(k + 2)
        alpha1, psum1, m = exp_stage(s_ref.at[b], p_ref.at[b], tile_max, m)
        l = value_stage(k, p_ref.at[a], alpha0, psum0, l)
        return alpha1, psum1, m, l

    def pair(kk, carry):
        return step(2 * kk + 1, 1, step(2 * kk, 0, carry))

    n_steps = n_tiles - 1
    carry = lax.fori_loop(0, n_steps // 2, pair, carry)
    carry = lax.cond(n_steps % 2 == 1, lambda c: step(n_steps - 1, 0, c), lambda c: c, carry)
    alpha0, psum0, _, l = carry
    l = value_stage(n_steps, p_ref.at[n_steps % 2], alpha0, psum0, l)
    o_ref[...] = jnp.transpose(acc_ref[...] / l).astype(o_ref.dtype)


def _flash(q_arr, q_off, qx_arr, qx_off, k_arr, k_off, kx_arr, vt_arr, name, tq, tk):
    return pl.pallas_call(
        functools.partial(_flash_kernel, tq=tq, tk=tk),
        grid=(HEADS, SEQ // tq),
        in_specs=[pl.BlockSpec((tq, HEAD_DIM), lambda h, i: (i, q_off + h)),
                  pl.BlockSpec((tq, LANES), lambda h, i: (i, qx_off + h)),
                  pl.BlockSpec((SEQ, HEAD_DIM), lambda h, i: (0, k_off + h)),
                  pl.BlockSpec((SEQ, LANES), lambda h, i: (0, 0)),
                  pl.BlockSpec((1, SEQ // tk, HEAD_DIM, tk), lambda h, i: (h, 0, 0, 0))],
        out_specs=pl.BlockSpec((tq, HEAD_DIM), lambda h, i: (i, h)),
        out_shape=jax.ShapeDtypeStruct((SEQ, WIDTH), BF16),
        scratch_shapes=[pltpu.VMEM((2, tk, tq), F32),
                        pltpu.VMEM((2, tk, tq), BF16),
                        pltpu.VMEM((HEAD_DIM, tq), F32)],
        compiler_params=_params("parallel", "arbitrary"),
        name=name,
    )(q_arr, qx_arr, k_arr, kx_arr, vt_arr)


def _gates_kernel(x_ref, g_ref, w_ref, o_ref, xn_ref):
    @pl.when(pl.program_id(1) == 0)
    def _():
        xn_ref[...] = _rmsnorm_bf16(x_ref[...], g_ref[...])

    z = jnp.dot(xn_ref[...], w_ref[...], preferred_element_type=F32)
    o_ref[...] = jax.nn.sigmoid(z)


def _gates(x, g, w_g, tm=512, tn=1024):
    n = w_g.shape[1]
    return pl.pallas_call(
        _gates_kernel,
        grid=(SEQ // tm, n // tn),
        in_specs=[pl.BlockSpec((tm, D_MODEL), lambda i, j: (i, 0)),
                  pl.BlockSpec((1, D_MODEL), lambda i, j: (0, 0)),
                  pl.BlockSpec((D_MODEL, tn), lambda i, j: (0, j))],
        out_specs=pl.BlockSpec((tm, tn), lambda i, j: (i, j)),
        out_shape=jax.ShapeDtypeStruct((SEQ, n), F32),
        scratch_shapes=[pltpu.VMEM((tm, D_MODEL), BF16)],
        compiler_params=_params("parallel", "arbitrary"),
        name="gates",
    )(x, g, w_g)


def _mix1_kernel(ya_ref, yb_ref, wa_ref, wb_ref, ga_ref, gb_ref, o_ref):
    a = jnp.dot(ya_ref[...], wa_ref[...], preferred_element_type=F32)
    b = jnp.dot(yb_ref[...], wb_ref[...], preferred_element_type=F32)
    o_ref[...] = (ga_ref[...] * a + gb_ref[...] * b).astype(BF16)


def _mix1(ya, yb, wa, wb, gates, tm=512, tn=1024):
    nj = D_MODEL // tn
    return pl.pallas_call(
        _mix1_kernel,
        grid=(SEQ // tm, nj),
        in_specs=[pl.BlockSpec((tm, WIDTH), lambda i, j: (i, 0)),
                  pl.BlockSpec((tm, WIDTH), lambda i, j: (i, 0)),
                  pl.BlockSpec((WIDTH, tn), lambda i, j: (0, j)),
                  pl.BlockSpec((WIDTH, tn), lambda i, j: (0, j)),
                  pl.BlockSpec((tm, tn), lambda i, j: (i, j)),
                  pl.BlockSpec((tm, tn), lambda i, j: (i, j + nj))],
        out_specs=pl.BlockSpec((tm, tn), lambda i, j: (i, j)),
        out_shape=jax.ShapeDtypeStruct((SEQ, D_MODEL), BF16),
        compiler_params=_params("parallel", "arbitrary"),
        name="mix1",
    )(ya, yb, wa, wb, gates, gates)


def _mix2_kernel(m_ref, w_ref, x_ref, o_ref):
    o_ref[...] = x_ref[...] + jnp.dot(m_ref[...], w_ref[...], preferred_element_type=F32)


def _mix2(mixed, w_out, x, tm=512, tn=1024):
    return pl.pallas_call(
        _mix2_kernel,
        grid=(SEQ // tm, D_MODEL // tn),
        in_specs=[pl.BlockSpec((tm, D_MODEL), lambda i, j: (i, 0)),
                  pl.BlockSpec((D_MODEL, tn), lambda i, j: (0, j)),
                  pl.BlockSpec((tm, tn), lambda i, j: (i, j))],
        out_specs=pl.BlockSpec((tm, tn), lambda i, j: (i, j)),
        out_shape=jax.ShapeDtypeStruct((SEQ, D_MODEL), F32),
        compiler_params=_params("parallel", "arbitrary"),
        name="mix2",
    )(mixed, w_out, x)


def _ffn_kernel(h_ref, g_ref, wg_ref, wu_ref, wd_ref, fg_ref, o_ref, hn_ref, acc_ref):
    f = pl.program_id(1)

    @pl.when(f == 0)
    def _():
        hn_ref[...] = _rmsnorm_bf16(h_ref[...], g_ref[...])
        acc_ref[...] = jnp.zeros_like(acc_ref)

    hn = hn_ref[...]
    gate = jnp.dot(hn, wg_ref[...], preferred_element_type=F32)
    up = jnp.dot(hn, wu_ref[...], preferred_element_type=F32)
    act = (jax.nn.silu(gate) * up).astype(BF16)
    acc_ref[...] += jnp.dot(act, wd_ref[...], preferred_element_type=F32)

    @pl.when(f == pl.num_programs(1) - 1)
    def _():
        y = h_ref[...] + acc_ref[...]
        ms = jnp.mean(y * y, axis=-1, keepdims=True)
        o_ref[...] = (y * lax.rsqrt(ms + EPS)) * fg_ref[...]


def _ffn(h, g, wg, wu, wd, fg, tm=512, tf=512):
    return pl.pallas_call(
        _ffn_kernel,
        grid=(SEQ // tm, D_FF // tf),
        in_specs=[pl.BlockSpec((tm, D_MODEL), lambda i, f: (i, 0)),
                  pl.BlockSpec((1, D_MODEL), lambda i, f: (0, 0)),
                  pl.BlockSpec((D_MODEL, tf), lambda i, f: (0, f)),
                  pl.BlockSpec((D_MODEL, tf), lambda i, f: (0, f)),
                  pl.BlockSpec((tf, D_MODEL), lambda i, f: (f, 0)),
                  pl.BlockSpec((1, D_MODEL), lambda i, f: (0, 0))],
        out_specs=pl.BlockSpec((tm, D_MODEL), lambda i, f: (i, 0)),
        out_shape=jax.ShapeDtypeStruct((SEQ, D_MODEL), F32),
        scratch_shapes=[pltpu.VMEM((tm, D_MODEL), BF16),
                        pltpu.VMEM((tm, D_MODEL), F32)],
        compiler_params=_params("parallel", "arbitrary"),
        name="ffn",
    )(h, g, wg, wu, wd, fg)


def _rope_freqs():
    def inv_freq(d):
        return ROPE_THETA ** (-jnp.arange(0, d, 2, dtype=F32) / d)

    fa = inv_freq(2 * MOBA_ROPE_HALF)
    fb = inv_freq(2 * MLA_ROPE_HALF)
    pad = jnp.zeros((LANES - 2 * MOBA_ROPE_HALF - 2 * MLA_ROPE_HALF,), F32)
    return jnp.concatenate([fa, fa, fb, fb, pad]).reshape(1, LANES)


def kernel(x, positions, attn_norm, w_in, q_norm, w_uq, kv_norm, w_ukv, w_branch_a, w_branch_b,
           w_out, ffn_norm, w_gate, w_up, w_down, final_norm):
    assert x.shape == (1, SEQ, D_MODEL) and w_in.shape[0] == 1
    x2 = x.reshape(SEQ, D_MODEL)
    pos = positions.reshape(SEQ, 1)
    row = lambda v: v.reshape(1, -1).astype(F32)

    w = w_in[0]
    c0 = 3 * WIDTH
    c1 = c0 + MLA_Q_RANK + MLA_KV_RANK + MLA_ROPE_DIM
    w_qk = w[:, :2 * WIDTH].astype(BF16)
    w_vt = w[:, 2 * WIDTH:c0].T.astype(BF16)
    w_c = jnp.pad(w[:, c0:c1], ((0, 0), (0, LANES - MLA_ROPE_DIM))).astype(BF16)
    w_g = w[:, c1:].astype(BF16)
    uq = w_uq[0].reshape(MLA_Q_RANK, HEADS, HEAD_DIM + MLA_ROPE_DIM)
    uq_rope = jnp.pad(uq[:, :, HEAD_DIM:], ((0, 0), (0, 0), (0, LANES - MLA_ROPE_DIM)))
    w_uq2 = jnp.concatenate([uq[:, :, :HEAD_DIM].reshape(MLA_Q_RANK, WIDTH),
                             uq_rope.reshape(MLA_Q_RANK, WIDTH)], axis=1).astype(BF16)
    ukv = w_ukv[0].reshape(MLA_KV_RANK, HEADS, 2 * HEAD_DIM)
    w_uk = ukv[:, :, :HEAD_DIM].reshape(MLA_KV_RANK, WIDTH).astype(BF16)
    w_uvt = ukv[:, :, HEAD_DIM:].reshape(MLA_KV_RANK, WIDTH).T.astype(BF16)

    cos_a, sin_a, cos_b, sin_b = _rope_tables(pos, _rope_freqs())

    g_attn = row(attn_norm[0])
    qk_a, vt_a, kmean = _qkv_proj(x2, g_attn, w_qk, w_vt, cos_a, sin_a, FLASH_TK)
    qb, kn_b, vt_b, krx = _mla_proj(x2, g_attn, w_c, row(q_norm[0]), w_uq2, row(kv_norm[0]),
                                    w_uk, w_uvt, cos_b, sin_b, FLASH_TK)

    kmean_pad = jnp.pad(kmean.reshape(N_BLOCKS, WIDTH), ((0, LANES - N_BLOCKS), (0, 0)))
    bias = _moba_select(qk_a, kmean_pad)
    blk = jnp.arange(SEQ, dtype=jnp.int32)[:, None] // MOBA_BLOCK
    onehot = (blk == jnp.arange(LANES, dtype=jnp.int32)[None, :]).astype(BF16)

    y_a = _flash(qk_a, 0, bias, 0, qk_a, HEADS, onehot, vt_a, "flash_moba", FLASH_TQ, FLASH_TK)
    y_b = _flash(qb, 0, qb, HEADS, kn_b, 0, krx, vt_b, "flash_mla", FLASH_TQ, FLASH_TK)

    gates = _gates(x2, g_attn, w_g)
    mixed = _mix1(y_a, y_b, w_branch_a[0].astype(BF16), w_branch_b[0].astype(BF16), gates)
    h = _mix2(mixed, w_out[0].astype(BF16), x2)
    out = _ffn(h, row(ffn_norm[0]), w_gate[0].astype(BF16), w_up[0].astype(BF16),
               w_down[0].astype(BF16), row(final_norm))
    return out.reshape(1, SEQ, D_MODEL)
```

```python
import functools

import jax
import jax.numpy as jnp
from jax import lax
from jax.experimental import pallas as pl
from jax.experimental.pallas import tpu as pltpu

F32 = jnp.float32
BF16 = jnp.bfloat16

D_MODEL = 2048
SEQ = 16384
HEADS = 8
HEAD_DIM = 128
WIDTH = HEADS * HEAD_DIM
MOBA_BLOCK = 256
MOBA_TOPK = 3
N_BLOCKS = SEQ // MOBA_BLOCK
MOBA_ROPE_HALF = 16
MLA_ROPE_DIM = 64
MLA_ROPE_HALF = 32
MLA_Q_RANK = 512
MLA_KV_RANK = 256
ROPE_THETA = 500000.0
D_FF = 5632
EPS = 1e-6
NEG = -1e30
LANES = 128
SUBLANES = 8
FLASH_TQ = 1024
FLASH_TK = 512
FLASH_CHUNK = 64
MERGE_CHUNK = 512

LOG2E = 1.4426950408889634
MOBA_SCALE = HEAD_DIM ** -0.5 * LOG2E
MLA_SCALE = (HEAD_DIM + MLA_ROPE_DIM) ** -0.5 * LOG2E

VMEM_LIMIT = 52 * 1024 * 1024


def _params(*sem, flags=None):
    return pltpu.CompilerParams(dimension_semantics=sem, vmem_limit_bytes=VMEM_LIMIT, flags=flags)


def _rmsnorm_bf16(x, g):
    ms = jnp.mean(x * x, axis=-1, keepdims=True)
    return ((x * lax.rsqrt(ms + EPS)) * g).astype(BF16)


def _rope(x, cos, sin, half):
    lane = lax.broadcasted_iota(jnp.int32, x.shape, 1)
    partner = jnp.where(lane < half, pltpu.roll(x, LANES - half, 1), pltpu.roll(x, half, 1))
    return x * cos + partner * sin


def _rope_table_kernel(pos_ref, f_ref, ca_ref, sa_ref, cb_ref, sb_ref):
    ang = pos_ref[...].astype(F32) * f_ref[...]
    c = jnp.cos(ang)
    s = jnp.sin(ang)
    lane = lax.broadcasted_iota(jnp.int32, c.shape, 1)
    ha, hb = MOBA_ROPE_HALF, MLA_ROPE_HALF
    ca_ref[...] = jnp.where(lane < 2 * ha, c, 1.0)
    sa_ref[...] = jnp.where(lane < ha, -s, jnp.where(lane < 2 * ha, s, 0.0))
    c2 = pltpu.roll(c, LANES - 2 * ha, 1)
    s2 = pltpu.roll(s, LANES - 2 * ha, 1)
    cb_ref[...] = jnp.where(lane < 2 * hb, c2, 1.0)
    sb_ref[...] = jnp.where(lane < hb, -s2, jnp.where(lane < 2 * hb, s2, 0.0))


def _rope_tables(pos, freqs, tm=2048):
    out = jax.ShapeDtypeStruct((SEQ, LANES), F32)
    row = pl.BlockSpec((tm, LANES), lambda i: (i, 0))
    return pl.pallas_call(
        _rope_table_kernel,
        grid=(SEQ // tm,),
        in_specs=[pl.BlockSpec((tm, 1), lambda i: (i, 0)),
                  pl.BlockSpec((1, LANES), lambda i: (0, 0))],
        out_specs=[row, row, row, row],
        out_shape=[out, out, out, out],
        compiler_params=_params("parallel"),
        name="rope_tables",
    )(pos, freqs)


_NT = (((1,), (1,)), ((), ()))


def _store_vt(vt_ref, vt):
    for h in range(HEADS):
        vt_ref[h, 0] = vt[h * HEAD_DIM:(h + 1) * HEAD_DIM, :].astype(BF16)


def _qkv_kernel(x_ref, g_ref, w_ref, wvt_ref, cos_ref, sin_ref, o_ref, vt_ref, kmean_ref, xn_ref):
    j = pl.program_id(1)
    tm = x_ref.shape[0]

    @pl.when(j == 0)
    def _():
        xn_ref[...] = _rmsnorm_bf16(x_ref[...], g_ref[...])

    def roped_heads():
        acc = jnp.dot(xn_ref[...], w_ref[...], preferred_element_type=F32)
        cos, sin = cos_ref[...], sin_ref[...]
        for h in range(HEADS):
            sl = slice(h * HEAD_DIM, (h + 1) * HEAD_DIM)
            yield sl, _rope(acc[:, sl], cos, sin, MOBA_ROPE_HALF)

    @pl.when(j == 0)
    def _():
        for sl, r in roped_heads():
            o_ref[:, sl] = (r * MOBA_SCALE).astype(BF16)

    @pl.when(j == 1)
    def _():
        for sl, r in roped_heads():
            o_ref[:, sl] = r.astype(BF16)
            for b in range(tm // MOBA_BLOCK):
                rows = r[b * MOBA_BLOCK:(b + 1) * MOBA_BLOCK]
                kmean_ref[0, b:b + 1, sl] = jnp.mean(rows, axis=0, keepdims=True)

    @pl.when(j == 2)
    def _():
        vt = lax.dot_general(wvt_ref[...], xn_ref[...], _NT, preferred_element_type=F32)
        _store_vt(vt_ref, vt)


def _qkv_proj(x, g, w_qk, w_vt, cos_a, sin_a, tm):
    nb = tm // MOBA_BLOCK
    return pl.pallas_call(
        _qkv_kernel,
        grid=(SEQ // tm, 3),
        in_specs=[pl.BlockSpec((tm, D_MODEL), lambda i, j: (i, 0)),
                  pl.BlockSpec((1, D_MODEL), lambda i, j: (0, 0)),
                  pl.BlockSpec((D_MODEL, WIDTH), lambda i, j: (0, jnp.minimum(j, 1))),
                  pl.BlockSpec((WIDTH, D_MODEL), lambda i, j: (0, 0)),
                  pl.BlockSpec((tm, LANES), lambda i, j: (i, 0)),
                  pl.BlockSpec((tm, LANES), lambda i, j: (i, 0))],
        out_specs=[pl.BlockSpec((tm, WIDTH), lambda i, j: (i, jnp.minimum(j, 1))),
                   pl.BlockSpec((HEADS, 1, HEAD_DIM, tm), lambda i, j: (0, i, 0, 0)),
                   pl.BlockSpec((1, nb, WIDTH), lambda i, j: (i, 0, 0))],
        out_shape=[jax.ShapeDtypeStruct((SEQ, 2 * WIDTH), BF16),
                   jax.ShapeDtypeStruct((HEADS, SEQ // tm, HEAD_DIM, tm), BF16),
                   jax.ShapeDtypeStruct((SEQ // tm, nb, WIDTH), F32)],
        scratch_shapes=[pltpu.VMEM((tm, D_MODEL), BF16)],
        compiler_params=_params("parallel", "arbitrary"),
        name="qkv_proj",
    )(x, g, w_qk, w_vt, cos_a, sin_a)


def _mla_kernel(x_ref, g_ref, wc_ref, qg_ref, wuq_ref, kvg_ref, wuk_ref, wuvt_ref, cos_ref, sin_ref,
                qb_ref, kn_ref, vt_ref, kr_ref):
    xn = _rmsnorm_bf16(x_ref[...], g_ref[...])
    c = jnp.dot(xn, wc_ref[...], preferred_element_type=F32)
    cos, sin = cos_ref[...], sin_ref[...]
    kr = c[:, MLA_Q_RANK + MLA_KV_RANK:]
    kr_ref[...] = _rope(kr, cos, sin, MLA_ROPE_HALF).astype(BF16)

    cq = _rmsnorm_bf16(c[:, :MLA_Q_RANK], qg_ref[...])
    q = jnp.dot(cq, wuq_ref[...], preferred_element_type=F32)
    qb_ref[:, :WIDTH] = (q[:, :WIDTH] * MLA_SCALE).astype(BF16)
    for h in range(HEADS):
        sl = slice(WIDTH + h * HEAD_DIM, WIDTH + (h + 1) * HEAD_DIM)
        qb_ref[:, sl] = (_rope(q[:, sl], cos, sin, MLA_ROPE_HALF) * MLA_SCALE).astype(BF16)

    ckv = _rmsnorm_bf16(c[:, MLA_Q_RANK:MLA_Q_RANK + MLA_KV_RANK], kvg_ref[...])
    kn_ref[...] = jnp.dot(ckv, wuk_ref[...], preferred_element_type=F32).astype(BF16)
    _store_vt(vt_ref, lax.dot_general(wuvt_ref[...], ckv, _NT, preferred_element_type=F32))


def _mla_proj(x, g, w_c, qg, w_uq, kvg, w_uk, w_uvt, cos_b, sin_b, tm):
    nc = w_c.shape[1]
    full = lambda shape: pl.BlockSpec(shape, lambda i: (0, 0))
    return pl.pallas_call(
        _mla_kernel,
        grid=(SEQ // tm,),
        in_specs=[pl.BlockSpec((tm, D_MODEL), lambda i: (i, 0)),
                  full((1, D_MODEL)), full((D_MODEL, nc)),
                  full((1, MLA_Q_RANK)), full((MLA_Q_RANK, 2 * WIDTH)),
                  full((1, MLA_KV_RANK)), full((MLA_KV_RANK, WIDTH)), full((WIDTH, MLA_KV_RANK)),
                  pl.BlockSpec((tm, LANES), lambda i: (i, 0)),
                  pl.BlockSpec((tm, LANES), lambda i: (i, 0))],
        out_specs=[pl.BlockSpec((tm, 2 * WIDTH), lambda i: (i, 0)),
                   pl.BlockSpec((tm, WIDTH), lambda i: (i, 0)),
                   pl.BlockSpec((HEADS, 1, HEAD_DIM, tm), lambda i: (0, i, 0, 0)),
                   pl.BlockSpec((tm, LANES), lambda i: (i, 0))],
        out_shape=[jax.ShapeDtypeStruct((SEQ, 2 * WIDTH), BF16),
                   jax.ShapeDtypeStruct((SEQ, WIDTH), BF16),
                   jax.ShapeDtypeStruct((HEADS, SEQ // tm, HEAD_DIM, tm), BF16),
                   jax.ShapeDtypeStruct((SEQ, LANES), BF16)],
        compiler_params=_params("parallel"),
        name="mla_proj",
    )(x, g, w_c, qg, w_uq, kvg, w_uk, w_uvt, cos_b, sin_b)


def _select_kernel(q_ref, kmean_ref, bias_ref):
    q = q_ref[...].astype(F32)
    gate = lax.dot_general(kmean_ref[...], q, _NT, preferred_element_type=F32,
                           precision=lax.Precision.HIGHEST)
    blk = lax.broadcasted_iota(jnp.int32, gate.shape, 0)
    qry = lax.broadcasted_iota(jnp.int32, gate.shape, 1)
    own = (pl.program_id(1) * gate.shape[1] + qry) // MOBA_BLOCK
    past = blk < own
    g = jnp.where(past, gate, -jnp.inf)
    sel = blk == own
    for _ in range(MOBA_TOPK):
        mx = jnp.max(g, axis=0, keepdims=True)
        cand = jnp.where((g == mx) & past, blk, LANES)
        idx = jnp.min(cand, axis=0, keepdims=True)
        hit = blk == idx
        sel = sel | hit
        g = jnp.where(hit, -jnp.inf, g)
    bias_t = jnp.where(sel, 0.0, jnp.where(blk < N_BLOCKS, NEG, 0.0))
    bias_ref[...] = jnp.transpose(bias_t).astype(BF16)


def _moba_select(qkv, kmean_pad, ts=2048):
    return pl.pallas_call(
        _select_kernel,
        grid=(HEADS, SEQ // ts),
        in_specs=[pl.BlockSpec((ts, HEAD_DIM), lambda h, i: (i, h)),
                  pl.BlockSpec((LANES, HEAD_DIM), lambda h, i: (0, h))],
        out_specs=pl.BlockSpec((ts, LANES), lambda h, i: (i, h)),
        out_shape=jax.ShapeDtypeStruct((SEQ, WIDTH), BF16),
        compiler_params=_params("parallel", "parallel"),
        name="moba_select",
    )(qkv, kmean_pad)


def _flash_kernel(q_ref, qx_ref, k_ref, kx_ref, vt_ref, o_ref,
                  s0_ref, s1_ref, p0_ref, p1_ref, acc_ref, st_ref, *, tq, tk):
    i = pl.program_id(1)
    diag = tq // tk
    n_tiles = (i + 1) * diag
    qa = jnp.concatenate([q_ref[...], qx_ref[...]], axis=1)

    def tile_of(n):
        return jnp.where(n < diag, i * diag + n, n - diag)

    def scores(n):
        rows = pl.ds(pl.multiple_of(tile_of(n) * tk, tk), tk)
        ka = jnp.concatenate([k_ref[rows, :], kx_ref[rows, :]], axis=1)
        return lax.dot_general(ka, qa, _NT, preferred_element_type=F32)

    def diag_scores(n):
        s = scores(n)
        key = lax.broadcasted_iota(jnp.int32, s.shape, 0) + n * tk
        qry = lax.broadcasted_iota(jnp.int32, s.shape, 1)
        return jnp.where(key <= qry, s, NEG)

    chunks = [slice(c * FLASH_CHUNK, (c + 1) * FLASH_CHUNK) for c in range(tk // FLASH_CHUNK)]

    def fold8(x, op):
        out = x[0:SUBLANES]
        for r in range(1, x.shape[0] // SUBLANES):
            out = op(out, x[r * SUBLANES:(r + 1) * SUBLANES])
        return out

    def rep(x8, rows):
        return jnp.concatenate([x8] * (rows // SUBLANES), axis=0)

    ST_M, ST_L, ST_MAX, ST_ALPHA, ST_PSUM = 0, 1, 2, 4, 6
    s_slot, p_slot = (s0_ref, s1_ref), (p0_ref, p1_ref)

    def score_stage(n, slot, masked=False):
        val = diag_scores(n) if masked else scores(n)
        s_slot[slot][...] = val
        mx = fold8(val[chunks[0], :], jnp.maximum)
        for c in chunks[1:]:
            mx = jnp.maximum(mx, fold8(val[c, :], jnp.maximum))
        st_ref[ST_MAX + slot] = jnp.broadcast_to(jnp.max(mx, axis=0, keepdims=True), mx.shape)

    def exp_stage(slot):
        m = st_ref[ST_M]
        m_new = jnp.maximum(m, st_ref[ST_MAX + slot])
        st_ref[ST_M] = m_new
        st_ref[ST_ALPHA + slot] = jnp.exp2(m - m_new)
        m_rep = rep(m_new, FLASH_CHUNK)
        psum = jnp.zeros((SUBLANES, tq), F32)
        for c in chunks:
            pc = jnp.exp2(s_slot[slot][c, :] - m_rep)
            p_slot[slot][c, :] = pc.astype(BF16)
            psum = psum + fold8(pc, jnp.add)
        st_ref[ST_PSUM + slot] = psum

    def value_stage(n, slot):
        alpha = st_ref[ST_ALPHA + slot]
        pv = jnp.dot(vt_ref[0, tile_of(n)], p_slot[slot][...], preferred_element_type=F32)
        acc_ref[...] = rep(alpha, HEAD_DIM) * acc_ref[...] + pv
        st_ref[ST_L] = alpha * st_ref[ST_L] + st_ref[ST_PSUM + slot]

    assert diag == 2
    score_stage(0, 0, masked=True)
    score_stage(1, 1, masked=True)
    acc_ref[...] = jnp.zeros_like(acc_ref)
    st_ref[ST_M] = jnp.full((SUBLANES, tq), NEG, F32)
    st_ref[ST_L] = jnp.zeros((SUBLANES, tq), F32)
    exp_stage(0)

    def step(k, a, last=False):
        if not last:
            score_stage(k + 2, a)
        exp_stage(1 - a)
        value_stage(k, a)

    def pair(kk, carry):
        step(2 * kk, 0)
        step(2 * kk + 1, 1)
        return carry

    lax.fori_loop(0, i, pair, 0)
    step(n_tiles - 2, 0, last=True)
    value_stage(n_tiles - 1, 1)
    l = jnp.sum(st_ref[ST_L], axis=0, keepdims=True)
    o_ref[...] = jnp.transpose(acc_ref[...] / l).astype(o_ref.dtype)


def _flash(q_arr, q_off, qx_arr, qx_off, k_arr, k_off, kx_arr, vt_arr, name, tq, tk):
    return pl.pallas_call(
        functools.partial(_flash_kernel, tq=tq, tk=tk),
        grid=(HEADS, SEQ // tq),
        in_specs=[pl.BlockSpec((tq, HEAD_DIM), lambda h, i: (i, q_off + h)),
                  pl.BlockSpec((tq, LANES), lambda h, i: (i, qx_off + h)),
                  pl.BlockSpec((SEQ, HEAD_DIM), lambda h, i: (0, k_off + h)),
                  pl.BlockSpec((SEQ, LANES), lambda h, i: (0, 0)),
                  pl.BlockSpec((1, SEQ // tk, HEAD_DIM, tk), lambda h, i: (h, 0, 0, 0))],
        out_specs=pl.BlockSpec((tq, HEAD_DIM), lambda h, i: (i, h)),
        out_shape=jax.ShapeDtypeStruct((SEQ, WIDTH), BF16),
        scratch_shapes=[pltpu.VMEM((tk, tq), F32), pltpu.VMEM((tk, tq), F32),
                        pltpu.VMEM((tk, tq), BF16), pltpu.VMEM((tk, tq), BF16),
                        pltpu.VMEM((HEAD_DIM, tq), F32),
                        pltpu.VMEM((8, SUBLANES, tq), F32)],
        compiler_params=_params("parallel", "arbitrary"),
        name=name,
    )(q_arr, qx_arr, k_arr, kx_arr, vt_arr)


def _resident(shape):
    return pl.BlockSpec(shape, lambda i: (0,) * len(shape), pipeline_mode=pl.Buffered(1))


def _merge_kernel(x_ref, g_ref, wg_ref, ya_ref, yb_ref, wa_ref, wb_ref, o_ref):
    xn = _rmsnorm_bf16(x_ref[...], g_ref[...])
    ya, yb = ya_ref[...], yb_ref[...]
    for c in range(D_MODEL // MERGE_CHUNK):
        cols = slice(c * MERGE_CHUNK, (c + 1) * MERGE_CHUNK)
        cols_b = slice(D_MODEL + c * MERGE_CHUNK, D_MODEL + (c + 1) * MERGE_CHUNK)
        gate_a = jax.nn.sigmoid(jnp.dot(xn, wg_ref[:, cols], preferred_element_type=F32))
        gate_b = jax.nn.sigmoid(jnp.dot(xn, wg_ref[:, cols_b], preferred_element_type=F32))
        a = jnp.dot(ya, wa_ref[:, cols], preferred_element_type=F32)
        b = jnp.dot(yb, wb_ref[:, cols], preferred_element_type=F32)
        o_ref[:, cols] = (gate_a * a + gate_b * b).astype(BF16)


def _gated_merge(x, g, w_g, ya, yb, wa, wb, tm=512):
    return pl.pallas_call(
        _merge_kernel,
        grid=(SEQ // tm,),
        in_specs=[pl.BlockSpec((tm, D_MODEL), lambda i: (i, 0)),
                  _resident((1, D_MODEL)), _resident((D_MODEL, 2 * D_MODEL)),
                  pl.BlockSpec((tm, WIDTH), lambda i: (i, 0)),
                  pl.BlockSpec((tm, WIDTH), lambda i: (i, 0)),
                  _resident((WIDTH, D_MODEL)), _resident((WIDTH, D_MODEL))],
        out_specs=pl.BlockSpec((tm, D_MODEL), lambda i: (i, 0)),
        out_shape=jax.ShapeDtypeStruct((SEQ, D_MODEL), BF16),
        compiler_params=_params("parallel"),
        name="gated_merge",
    )(x, g, w_g, ya, yb, wa, wb)


def _out_kernel(m_ref, w_ref, x_ref, o_ref):
    o_ref[...] = x_ref[...] + jnp.dot(m_ref[...], w_ref[...], preferred_element_type=F32)


def _out_proj(mixed, w_out, x, tm=512):
    return pl.pallas_call(
        _out_kernel,
        grid=(SEQ // tm,),
        in_specs=[pl.BlockSpec((tm, D_MODEL), lambda i: (i, 0)),
                  _resident((D_MODEL, D_MODEL)),
                  pl.BlockSpec((tm, D_MODEL), lambda i: (i, 0))],
        out_specs=pl.BlockSpec((tm, D_MODEL), lambda i: (i, 0)),
        out_shape=jax.ShapeDtypeStruct((SEQ, D_MODEL), F32),
        compiler_params=_params("parallel"),
        name="out_proj",
    )(mixed, w_out, x)


def _ffn_kernel(h_ref, g_ref, wg_ref, wu_ref, wd_ref, fg_ref, o_ref, hn_ref, acc_ref):
    f = pl.program_id(1)

    @pl.when(f == 0)
    def _():
        hn_ref[...] = _rmsnorm_bf16(h_ref[...], g_ref[...])
        acc_ref[...] = jnp.zeros_like(acc_ref)

    hn = hn_ref[...]
    gate = jnp.dot(hn, wg_ref[...], preferred_element_type=F32)
    up = jnp.dot(hn, wu_ref[...], preferred_element_type=F32)
    act = (jax.nn.silu(gate) * up).astype(BF16)
    acc_ref[...] += jnp.dot(act, wd_ref[...], preferred_element_type=F32)

    @pl.when(f == pl.num_programs(1) - 1)
    def _():
        y = h_ref[...] + acc_ref[...]
        ms = jnp.mean(y * y, axis=-1, keepdims=True)
        o_ref[...] = (y * lax.rsqrt(ms + EPS)) * fg_ref[...]


def _ffn(h, g, wg, wu, wd, fg, tm=512, tf=512):
    return pl.pallas_call(
        _ffn_kernel,
        grid=(SEQ // tm, D_FF // tf),
        in_specs=[pl.BlockSpec((tm, D_MODEL), lambda i, f: (i, 0)),
                  pl.BlockSpec((1, D_MODEL), lambda i, f: (0, 0)),
                  pl.BlockSpec((D_MODEL, tf), lambda i, f: (0, f)),
                  pl.BlockSpec((D_MODEL, tf), lambda i, f: (0, f)),
                  pl.BlockSpec((tf, D_MODEL), lambda i, f: (f, 0)),
                  pl.BlockSpec((1, D_MODEL), lambda i, f: (0, 0))],
        out_specs=pl.BlockSpec((tm, D_MODEL), lambda i, f: (i, 0)),
        out_shape=jax.ShapeDtypeStruct((SEQ, D_MODEL), F32),
        scratch_shapes=[pltpu.VMEM((tm, D_MODEL), BF16),
                        pltpu.VMEM((tm, D_MODEL), F32)],
        compiler_params=_params("parallel", "arbitrary"),
        name="ffn",
    )(h, g, wg, wu, wd, fg)


def _rope_freqs():
    def inv_freq(d):
        return ROPE_THETA ** (-jnp.arange(0, d, 2, dtype=F32) / d)

    fa = inv_freq(2 * MOBA_ROPE_HALF)
    fb = inv_freq(2 * MLA_ROPE_HALF)
    pad = jnp.zeros((LANES - 2 * MOBA_ROPE_HALF - 2 * MLA_ROPE_HALF,), F32)
    return jnp.concatenate([fa, fa, fb, fb, pad]).reshape(1, LANES)


def kernel(x, positions, attn_norm, w_in, q_norm, w_uq, kv_norm, w_ukv, w_branch_a, w_branch_b,
           w_out, ffn_norm, w_gate, w_up, w_down, final_norm):
    assert x.shape == (1, SEQ, D_MODEL) and w_in.shape[0] == 1
    x2 = x.reshape(SEQ, D_MODEL)
    pos = positions.reshape(SEQ, 1)
    row = lambda v: v.reshape(1, -1).astype(F32)

    w = w_in[0]
    c0 = 3 * WIDTH
    c1 = c0 + MLA_Q_RANK + MLA_KV_RANK + MLA_ROPE_DIM
    w_qk = w[:, :2 * WIDTH].astype(BF16)
    w_vt = w[:, 2 * WIDTH:c0].T.astype(BF16)
    w_c = jnp.pad(w[:, c0:c1], ((0, 0), (0, LANES - MLA_ROPE_DIM))).astype(BF16)
    w_g = w[:, c1:].astype(BF16)
    uq = w_uq[0].reshape(MLA_Q_RANK, HEADS, HEAD_DIM + MLA_ROPE_DIM)
    uq_rope = jnp.pad(uq[:, :, HEAD_DIM:], ((0, 0), (0, 0), (0, LANES - MLA_ROPE_DIM)))
    w_uq2 = jnp.concatenate([uq[:, :, :HEAD_DIM].reshape(MLA_Q_RANK, WIDTH),
                             uq_rope.reshape(MLA_Q_RANK, WIDTH)], axis=1).astype(BF16)
    ukv = w_ukv[0].reshape(MLA_KV_RANK, HEADS, 2 * HEAD_DIM)
    w_uk = ukv[:, :, :HEAD_DIM].reshape(MLA_KV_RANK, WIDTH).astype(BF16)
    w_uvt = ukv[:, :, HEAD_DIM:].reshape(MLA_KV_RANK, WIDTH).T.astype(BF16)

    cos_a, sin_a, cos_b, sin_b = _rope_tables(pos, _rope_freqs())

    g_attn = row(attn_norm[0])
    qk_a, vt_a, kmean = _qkv_proj(x2, g_attn, w_qk, w_vt, cos_a, sin_a, FLASH_TK)
    qb, kn_b, vt_b, krx = _mla_proj(x2, g_attn, w_c, row(q_norm[0]), w_uq2, row(kv_norm[0]),
                                    w_uk, w_uvt, cos_b, sin_b, FLASH_TK)

    kmean_pad = jnp.pad(kmean.reshape(N_BLOCKS, WIDTH), ((0, LANES - N_BLOCKS), (0, 0)))
    bias = _moba_select(qk_a, kmean_pad)
    blk = jnp.arange(SEQ, dtype=jnp.int32)[:, None] // MOBA_BLOCK
    onehot = (blk == jnp.arange(LANES, dtype=jnp.int32)[None, :]).astype(BF16)

    y_a = _flash(qk_a, 0, bias, 0, qk_a, HEADS, onehot, vt_a, "flash_moba", FLASH_TQ, FLASH_TK)
    y_b = _flash(qb, 0, qb, HEADS, kn_b, 0, krx, vt_b, "flash_mla", FLASH_TQ, FLASH_TK)

    mixed = _gated_merge(x2, g_attn, w_g, y_a, y_b,
                         w_branch_a[0].astype(BF16), w_branch_b[0].astype(BF16))
    h = _out_proj(mixed, w_out[0].astype(BF16), x2)
    out = _ffn(h, row(ffn_norm[0]), w_gate[0].astype(BF16), w_up[0].astype(BF16),
               w_down[0].astype(BF16), row(final_norm))
    return out.reshape(1, SEQ, D_MODEL)
```

```python
import functools

import jax
import jax.numpy as jnp
from jax import lax
from jax.experimental import pallas as pl
from jax.experimental.pallas import tpu as pltpu

F32 = jnp.float32
BF16 = jnp.bfloat16

D_MODEL = 2048
SEQ = 16384
HEADS = 8
HEAD_DIM = 128
WIDTH = HEADS * HEAD_DIM
MOBA_BLOCK = 256
MOBA_TOPK = 3
N_BLOCKS = SEQ // MOBA_BLOCK
MOBA_ROPE_HALF = 16
MLA_ROPE_DIM = 64
MLA_ROPE_HALF = 32
MLA_Q_RANK = 512
MLA_KV_RANK = 256
ROPE_THETA = 500000.0
D_FF = 5632
EPS = 1e-6
NEG = -1e30
LANES = 128
SUBLANES = 8
FLASH_TQ = 1024
FLASH_TK = 512
FLASH_CHUNK = 64
MERGE_CHUNK = 512

LOG2E = 1.4426950408889634
MOBA_SCALE = HEAD_DIM ** -0.5 * LOG2E
MLA_SCALE = (HEAD_DIM + MLA_ROPE_DIM) ** -0.5 * LOG2E

VMEM_LIMIT = 52 * 1024 * 1024


def _params(*sem, flags=None):
    return pltpu.CompilerParams(dimension_semantics=sem, vmem_limit_bytes=VMEM_LIMIT, flags=flags)


def _rmsnorm_bf16(x, g):
    ms = jnp.mean(x * x, axis=-1, keepdims=True)
    return ((x * lax.rsqrt(ms + EPS)) * g).astype(BF16)


def _rope(x, cos, sin, half):
    lane = lax.broadcasted_iota(jnp.int32, x.shape, 1)
    partner = jnp.where(lane < half, pltpu.roll(x, LANES - half, 1), pltpu.roll(x, half, 1))
    return x * cos + partner * sin


def _rope_table_kernel(pos_ref, f_ref, ca_ref, sa_ref, cb_ref, sb_ref):
    ang = pos_ref[...].astype(F32) * f_ref[...]
    c = jnp.cos(ang)
    s = jnp.sin(ang)
    lane = lax.broadcasted_iota(jnp.int32, c.shape, 1)
    ha, hb = MOBA_ROPE_HALF, MLA_ROPE_HALF
    ca_ref[...] = jnp.where(lane < 2 * ha, c, 1.0)
    sa_ref[...] = jnp.where(lane < ha, -s, jnp.where(lane < 2 * ha, s, 0.0))
    c2 = pltpu.roll(c, LANES - 2 * ha, 1)
    s2 = pltpu.roll(s, LANES - 2 * ha, 1)
    cb_ref[...] = jnp.where(lane < 2 * hb, c2, 1.0)
    sb_ref[...] = jnp.where(lane < hb, -s2, jnp.where(lane < 2 * hb, s2, 0.0))


def _rope_tables(pos, freqs, tm=2048):
    out = jax.ShapeDtypeStruct((SEQ, LANES), F32)
    row = pl.BlockSpec((tm, LANES), lambda i: (i, 0))
    return pl.pallas_call(
        _rope_table_kernel,
        grid=(SEQ // tm,),
        in_specs=[pl.BlockSpec((tm, 1), lambda i: (i, 0)),
                  pl.BlockSpec((1, LANES), lambda i: (0, 0))],
        out_specs=[row, row, row, row],
        out_shape=[out, out, out, out],
        compiler_params=_params("parallel"),
        name="rope_tables",
    )(pos, freqs)


_NT = (((1,), (1,)), ((), ()))


def _store_vt(vt_ref, vt):
    for h in range(HEADS):
        vt_ref[h, 0] = vt[h * HEAD_DIM:(h + 1) * HEAD_DIM, :].astype(BF16)


def _qkv_kernel(x_ref, g_ref, w_ref, wvt_ref, cos_ref, sin_ref, o_ref, vt_ref, kmean_ref, xn_ref):
    j = pl.program_id(1)
    tm = x_ref.shape[0]

    @pl.when(j == 0)
    def _():
        xn_ref[...] = _rmsnorm_bf16(x_ref[...], g_ref[...])

    def roped_heads():
        acc = jnp.dot(xn_ref[...], w_ref[...], preferred_element_type=F32)
        cos, sin = cos_ref[...], sin_ref[...]
        for h in range(HEADS):
            sl = slice(h * HEAD_DIM, (h + 1) * HEAD_DIM)
            yield sl, _rope(acc[:, sl], cos, sin, MOBA_ROPE_HALF)

    @pl.when(j == 0)
    def _():
        for sl, r in roped_heads():
            o_ref[:, sl] = (r * MOBA_SCALE).astype(BF16)

    @pl.when(j == 1)
    def _():
        for sl, r in roped_heads():
            o_ref[:, sl] = r.astype(BF16)
            for b in range(tm // MOBA_BLOCK):
                rows = r[b * MOBA_BLOCK:(b + 1) * MOBA_BLOCK]
                kmean_ref[0, b:b + 1, sl] = jnp.mean(rows, axis=0, keepdims=True)

    @pl.when(j == 2)
    def _():
        vt = lax.dot_general(wvt_ref[...], xn_ref[...], _NT, preferred_element_type=F32)
        _store_vt(vt_ref, vt)


def _qkv_proj(x, g, w_qk, w_vt, cos_a, sin_a, tm):
    nb = tm // MOBA_BLOCK
    return pl.pallas_call(
        _qkv_kernel,
        grid=(SEQ // tm, 3),
        in_specs=[pl.BlockSpec((tm, D_MODEL), lambda i, j: (i, 0)),
                  pl.BlockSpec((1, D_MODEL), lambda i, j: (0, 0)),
                  pl.BlockSpec((D_MODEL, WIDTH), lambda i, j: (0, jnp.minimum(j, 1))),
                  pl.BlockSpec((WIDTH, D_MODEL), lambda i, j: (0, 0)),
                  pl.BlockSpec((tm, LANES), lambda i, j: (i, 0)),
                  pl.BlockSpec((tm, LANES), lambda i, j: (i, 0))],
        out_specs=[pl.BlockSpec((tm, WIDTH), lambda i, j: (i, jnp.minimum(j, 1))),
                   pl.BlockSpec((HEADS, 1, HEAD_DIM, tm), lambda i, j: (0, i, 0, 0)),
                   pl.BlockSpec((1, nb, WIDTH), lambda i, j: (i, 0, 0))],
        out_shape=[jax.ShapeDtypeStruct((SEQ, 2 * WIDTH), BF16),
                   jax.ShapeDtypeStruct((HEADS, SEQ // tm, HEAD_DIM, tm), BF16),
                   jax.ShapeDtypeStruct((SEQ // tm, nb, WIDTH), F32)],
        scratch_shapes=[pltpu.VMEM((tm, D_MODEL), BF16)],
        compiler_params=_params("parallel", "arbitrary"),
        name="qkv_proj",
    )(x, g, w_qk, w_vt, cos_a, sin_a)


def _mla_kernel(x_ref, g_ref, wc_ref, qg_ref, wuq_ref, kvg_ref, wuk_ref, wuvt_ref, cos_ref, sin_ref,
                qb_ref, kn_ref, vt_ref, kr_ref):
    xn = _rmsnorm_bf16(x_ref[...], g_ref[...])
    c = jnp.dot(xn, wc_ref[...], preferred_element_type=F32)
    cos, sin = cos_ref[...], sin_ref[...]
    kr = c[:, MLA_Q_RANK + MLA_KV_RANK:]
    kr_ref[...] = _rope(kr, cos, sin, MLA_ROPE_HALF).astype(BF16)

    cq = _rmsnorm_bf16(c[:, :MLA_Q_RANK], qg_ref[...])
    q = jnp.dot(cq, wuq_ref[...], preferred_element_type=F32)
    qb_ref[:, :WIDTH] = (q[:, :WIDTH] * MLA_SCALE).astype(BF16)
    for h in range(HEADS):
        sl = slice(WIDTH + h * HEAD_DIM, WIDTH + (h + 1) * HEAD_DIM)
        qb_ref[:, sl] = (_rope(q[:, sl], cos, sin, MLA_ROPE_HALF) * MLA_SCALE).astype(BF16)

    ckv = _rmsnorm_bf16(c[:, MLA_Q_RANK:MLA_Q_RANK + MLA_KV_RANK], kvg_ref[...])
    kn_ref[...] = jnp.dot(ckv, wuk_ref[...], preferred_element_type=F32).astype(BF16)
    _store_vt(vt_ref, lax.dot_general(wuvt_ref[...], ckv, _NT, preferred_element_type=F32))


def _mla_proj(x, g, w_c, qg, w_uq, kvg, w_uk, w_uvt, cos_b, sin_b, tm):
    nc = w_c.shape[1]
    full = lambda shape: pl.BlockSpec(shape, lambda i: (0, 0))
    return pl.pallas_call(
        _mla_kernel,
        grid=(SEQ // tm,),
        in_specs=[pl.BlockSpec((tm, D_MODEL), lambda i: (i, 0)),
                  full((1, D_MODEL)), full((D_MODEL, nc)),
                  full((1, MLA_Q_RANK)), full((MLA_Q_RANK, 2 * WIDTH)),
                  full((1, MLA_KV_RANK)), full((MLA_KV_RANK, WIDTH)), full((WIDTH, MLA_KV_RANK)),
                  pl.BlockSpec((tm, LANES), lambda i: (i, 0)),
                  pl.BlockSpec((tm, LANES), lambda i: (i, 0))],
        out_specs=[pl.BlockSpec((tm, 2 * WIDTH), lambda i: (i, 0)),
                   pl.BlockSpec((tm, WIDTH), lambda i: (i, 0)),
                   pl.BlockSpec((HEADS, 1, HEAD_DIM, tm), lambda i: (0, i, 0, 0)),
                   pl.BlockSpec((tm, LANES), lambda i: (i, 0))],
        out_shape=[jax.ShapeDtypeStruct((SEQ, 2 * WIDTH), BF16),
                   jax.ShapeDtypeStruct((SEQ, WIDTH), BF16),
                   jax.ShapeDtypeStruct((HEADS, SEQ // tm, HEAD_DIM, tm), BF16),
                   jax.ShapeDtypeStruct((SEQ, LANES), BF16)],
        compiler_params=_params("parallel"),
        name="mla_proj",
    )(x, g, w_c, qg, w_uq, kvg, w_uk, w_uvt, cos_b, sin_b)


def _select_kernel(q_ref, kmean_ref, bias_ref):
    q = q_ref[...].astype(F32)
    gate = lax.dot_general(kmean_ref[...], q, _NT, preferred_element_type=F32,
                           precision=lax.Precision.HIGHEST)
    blk = lax.broadcasted_iota(jnp.int32, gate.shape, 0)
    qry = lax.broadcasted_iota(jnp.int32, gate.shape, 1)
    own = (pl.program_id(1) * gate.shape[1] + qry) // MOBA_BLOCK
    past = blk < own
    g = jnp.where(past, gate, -jnp.inf)
    sel = blk == own
    for _ in range(MOBA_TOPK):
        mx = jnp.max(g, axis=0, keepdims=True)
        cand = jnp.where((g == mx) & past, blk, LANES)
        idx = jnp.min(cand, axis=0, keepdims=True)
        hit = blk == idx
        sel = sel | hit
        g = jnp.where(hit, -jnp.inf, g)
    bias_t = jnp.where(sel, 0.0, jnp.where(blk < N_BLOCKS, NEG, 0.0))
    bias_ref[...] = jnp.transpose(bias_t).astype(BF16)


def _moba_select(qkv, kmean_pad, ts=2048):
    return pl.pallas_call(
        _select_kernel,
        grid=(HEADS, SEQ // ts),
        in_specs=[pl.BlockSpec((ts, HEAD_DIM), lambda h, i: (i, h)),
                  pl.BlockSpec((LANES, HEAD_DIM), lambda h, i: (0, h))],
        out_specs=pl.BlockSpec((ts, LANES), lambda h, i: (i, h)),
        out_shape=jax.ShapeDtypeStruct((SEQ, WIDTH), BF16),
        compiler_params=_params("parallel", "parallel"),
        name="moba_select",
    )(qkv, kmean_pad)


def _flash_kernel(q_ref, qx_ref, k_ref, kx_ref, vt_ref, o_ref,
                  s0_ref, s1_ref, p0_ref, p1_ref, acc_ref,
                  m_ref, max0_ref, max1_ref, alpha0_ref, alpha1_ref, *, tq, tk):
    i = pl.program_id(1)
    diag = tq // tk
    n_tiles = (i + 1) * diag
    qa = jnp.concatenate([q_ref[...], qx_ref[...]], axis=1)

    def tile_of(n):
        return jnp.where(n < diag, i * diag + n, n - diag)

    def scores(n):
        rows = pl.ds(pl.multiple_of(tile_of(n) * tk, tk), tk)
        ka = jnp.concatenate([k_ref[rows, :], kx_ref[rows, :]], axis=1)
        return lax.dot_general(ka, qa, _NT, preferred_element_type=F32)

    def diag_scores(n):
        s = scores(n)
        key = lax.broadcasted_iota(jnp.int32, s.shape, 0) + n * tk
        qry = lax.broadcasted_iota(jnp.int32, s.shape, 1)
        return jnp.where(key <= qry, s, NEG)

    chunks = [slice(c * FLASH_CHUNK, (c + 1) * FLASH_CHUNK) for c in range(tk // FLASH_CHUNK)]

    def fold8(x, op):
        out = x[0:SUBLANES]
        for r in range(1, x.shape[0] // SUBLANES):
            out = op(out, x[r * SUBLANES:(r + 1) * SUBLANES])
        return out

    def rep(x8, rows):
        return jnp.concatenate([x8] * (rows // SUBLANES), axis=0)

    s_slot, p_slot = (s0_ref, s1_ref), (p0_ref, p1_ref)
    max_slot, alpha_slot = (max0_ref, max1_ref), (alpha0_ref, alpha1_ref)
    ones_rows = jnp.ones((acc_ref.shape[0] - HEAD_DIM, tk), BF16)

    def score_stage(n, slot, masked=False):
        val = diag_scores(n) if masked else scores(n)
        s_slot[slot][...] = val
        mx = fold8(val[chunks[0], :], jnp.maximum)
        for c in chunks[1:]:
            mx = jnp.maximum(mx, fold8(val[c, :], jnp.maximum))
        max_slot[slot][...] = jnp.broadcast_to(jnp.max(mx, axis=0, keepdims=True), mx.shape)

    def exp_stage(slot):
        m = m_ref[...]
        m_new = jnp.maximum(m, max_slot[slot][...])
        m_ref[...] = m_new
        alpha_slot[slot][...] = jnp.exp2(m - m_new)
        m_rep = rep(m_new, FLASH_CHUNK)
        for c in chunks:
            shifted = s_slot[slot][c, :] - m_rep
            p_slot[slot][c, :] = jnp.exp2(shifted.astype(BF16))

    def value_stage(n, slot):
        alpha = alpha_slot[slot][...]
        vt1 =jnp.concatenate([vt_ref[0, tile_of(n)], ones_rows], axis=0)
        pv = jnp.dot(vt1, p_slot[slot][...], preferred_element_type=F32)
        acc_ref[...] = rep(alpha, acc_ref.shape[0]) * acc_ref[...] + pv

    assert diag == 2
    score_stage(0, 0, masked=True)
    score_stage(1, 1, masked=True)
    acc_ref[...] = jnp.zeros_like(acc_ref)
    m_ref[...] = jnp.full((SUBLANES, tq), NEG, F32)
    exp_stage(0)

    def step(k, a, last=False):
        if not last:
            score_stage(k + 2, a)
        exp_stage(1 - a)
        value_stage(k, a)

    def pair(kk, carry):
        step(2 * kk, 0)
        step(2 * kk + 1, 1)
        return carry

    lax.fori_loop(0, i, pair, 0)
    step(n_tiles - 2, 0, last=True)
    value_stage(n_tiles - 1, 1)
    acc = acc_ref[...]
    out_t = acc[:HEAD_DIM] / acc[HEAD_DIM:HEAD_DIM + 1]
    o_ref[...] = jnp.transpose(out_t).astype(o_ref.dtype)


def _flash(q_arr, q_off, qx_arr, qx_off, k_arr, k_off, kx_arr, vt_arr, name, tq, tk):
    return pl.pallas_call(
        functools.partial(_flash_kernel, tq=tq, tk=tk),
        grid=(HEADS, SEQ // tq),
        in_specs=[pl.BlockSpec((tq, HEAD_DIM), lambda h, i: (i, q_off + h)),
                  pl.BlockSpec((tq, LANES), lambda h, i: (i, qx_off + h)),
                  pl.BlockSpec((SEQ, HEAD_DIM), lambda h, i: (0, k_off + h)),
                  pl.BlockSpec((SEQ, LANES), lambda h, i: (0, 0)),
                  pl.BlockSpec((1, SEQ // tk, HEAD_DIM, tk), lambda h, i: (h, 0, 0, 0))],
        out_specs=pl.BlockSpec((tq, HEAD_DIM), lambda h, i: (i, h)),
        out_shape=jax.ShapeDtypeStruct((SEQ, WIDTH), BF16),
        scratch_shapes=[pltpu.VMEM((tk, tq), F32), pltpu.VMEM((tk, tq), F32),
                        pltpu.VMEM((tk, tq), BF16), pltpu.VMEM((tk, tq), BF16),
                        pltpu.VMEM((HEAD_DIM + 2 * SUBLANES, tq), F32)]
        + [pltpu.VMEM((SUBLANES, tq), F32)] * 5,
        compiler_params=_params("parallel", "arbitrary"),
        name=name,
    )(q_arr, qx_arr, k_arr, kx_arr, vt_arr)


def _resident(shape):
    return pl.BlockSpec(shape, lambda i: (0,) * len(shape), pipeline_mode=pl.Buffered(1))


def _merge_kernel(x_ref, g_ref, wg_ref, ya_ref, yb_ref, wa_ref, wb_ref, o_ref):
    xn = _rmsnorm_bf16(x_ref[...], g_ref[...])
    ya, yb = ya_ref[...], yb_ref[...]
    for c in range(D_MODEL // MERGE_CHUNK):
        cols = slice(c * MERGE_CHUNK, (c + 1) * MERGE_CHUNK)
        cols_b = slice(D_MODEL + c * MERGE_CHUNK, D_MODEL + (c + 1) * MERGE_CHUNK)
        gate_a = jax.nn.sigmoid(jnp.dot(xn, wg_ref[:, cols], preferred_element_type=F32))
        gate_b = jax.nn.sigmoid(jnp.dot(xn, wg_ref[:, cols_b], preferred_element_type=F32))
        a = jnp.dot(ya, wa_ref[:, cols], preferred_element_type=F32)
        b = jnp.dot(yb, wb_ref[:, cols], preferred_element_type=F32)
        o_ref[:, cols] = (gate_a * a + gate_b * b).astype(BF16)


def _gated_merge(x, g, w_g, ya, yb, wa, wb, tm=512):
    return pl.pallas_call(
        _merge_kernel,
        grid=(SEQ // tm,),
        in_specs=[pl.BlockSpec((tm, D_MODEL), lambda i: (i, 0)),
                  _resident((1, D_MODEL)), _resident((D_MODEL, 2 * D_MODEL)),
                  pl.BlockSpec((tm, WIDTH), lambda i: (i, 0)),
                  pl.BlockSpec((tm, WIDTH), lambda i: (i, 0)),
                  _resident((WIDTH, D_MODEL)), _resident((WIDTH, D_MODEL))],
        out_specs=pl.BlockSpec((tm, D_MODEL), lambda i: (i, 0)),
        out_shape=jax.ShapeDtypeStruct((SEQ, D_MODEL), BF16),
        compiler_params=_params("parallel"),
        name="gated_merge",
    )(x, g, w_g, ya, yb, wa, wb)


def _out_kernel(m_ref, w_ref, x_ref, o_ref):
    o_ref[...] = x_ref[...] + jnp.dot(m_ref[...], w_ref[...], preferred_element_type=F32)


def _out_proj(mixed, w_out, x, tm=512):
    return pl.pallas_call(
        _out_kernel,
        grid=(SEQ // tm,),
        in_specs=[pl.BlockSpec((tm, D_MODEL), lambda i: (i, 0)),
                  _resident((D_MODEL, D_MODEL)),
                  pl.BlockSpec((tm, D_MODEL), lambda i: (i, 0))],
        out_specs=pl.BlockSpec((tm, D_MODEL), lambda i: (i, 0)),
        out_shape=jax.ShapeDtypeStruct((SEQ, D_MODEL), F32),
        compiler_params=_params("parallel"),
        name="out_proj",
    )(mixed, w_out, x)


def _ffn_kernel(h_ref, g_ref, wg_ref, wu_ref, wd_ref, fg_ref, o_ref, hn_ref, acc_ref):
    f = pl.program_id(1)

    @pl.when(f == 0)
    def _():
        hn_ref[...] = _rmsnorm_bf16(h_ref[...], g_ref[...])
        acc_ref[...] = jnp.zeros_like(acc_ref)

    hn = hn_ref[...]
    gate = jnp.dot(hn, wg_ref[...], preferred_element_type=F32)
    up = jnp.dot(hn, wu_ref[...], preferred_element_type=F32)
    act = (jax.nn.silu(gate) * up).astype(BF16)
    acc_ref[...] += jnp.dot(act, wd_ref[...], preferred_element_type=F32)

    @pl.when(f == pl.num_programs(1) - 1)
    def _():
        y = h_ref[...] + acc_ref[...]
        ms = jnp.mean(y * y, axis=-1, keepdims=True)
        o_ref[...] = (y * lax.rsqrt(ms + EPS)) * fg_ref[...]


def _ffn(h, g, wg, wu, wd, fg, tm=512, tf=512):
    return pl.pallas_call(
        _ffn_kernel,
        grid=(SEQ // tm, D_FF // tf),
        in_specs=[pl.BlockSpec((tm, D_MODEL), lambda i, f: (i, 0)),
                  pl.BlockSpec((1, D_MODEL), lambda i, f: (0, 0)),
                  pl.BlockSpec((D_MODEL, tf), lambda i, f: (0, f)),
                  pl.BlockSpec((D_MODEL, tf), lambda i, f: (0, f)),
                  pl.BlockSpec((tf, D_MODEL), lambda i, f: (f, 0)),
                  pl.BlockSpec((1, D_MODEL), lambda i, f: (0, 0))],
        out_specs=pl.BlockSpec((tm, D_MODEL), lambda i, f: (i, 0)),
        out_shape=jax.ShapeDtypeStruct((SEQ, D_MODEL), F32),
        scratch_shapes=[pltpu.VMEM((tm, D_MODEL), BF16),
                        pltpu.VMEM((tm, D_MODEL), F32)],
        compiler_params=_params("parallel", "arbitrary"),
        name="ffn",
    )(h, g, wg, wu, wd, fg)


def _rope_freqs():
    def inv_freq(d):
        return ROPE_THETA ** (-jnp.arange(0, d, 2, dtype=F32) / d)

    fa = inv_freq(2 * MOBA_ROPE_HALF)
    fb = inv_freq(2 * MLA_ROPE_HALF)
    pad = jnp.zeros((LANES - 2 * MOBA_ROPE_HALF - 2 * MLA_ROPE_HALF,), F32)
    return jnp.concatenate([fa, fa, fb, fb, pad]).reshape(1, LANES)


def kernel(x, positions, attn_norm, w_in, q_norm, w_uq, kv_norm, w_ukv, w_branch_a, w_branch_b,
           w_out, ffn_norm, w_gate, w_up, w_down, final_norm):
    assert x.shape == (1, SEQ, D_MODEL) and w_in.shape[0] == 1
    x2 = x.reshape(SEQ, D_MODEL)
    pos = positions.reshape(SEQ, 1)
    row = lambda v: v.reshape(1, -1).astype(F32)

    w = w_in[0]
    c0 = 3 * WIDTH
    c1 = c0 + MLA_Q_RANK + MLA_KV_RANK + MLA_ROPE_DIM
    w_qk = w[:, :2 * WIDTH].astype(BF16)
    w_vt = w[:, 2 * WIDTH:c0].T.astype(BF16)
    w_c = jnp.pad(w[:, c0:c1], ((0, 0), (0, LANES - MLA_ROPE_DIM))).astype(BF16)
    w_g = w[:, c1:].astype(BF16)
    uq = w_uq[0].reshape(MLA_Q_RANK, HEADS, HEAD_DIM + MLA_ROPE_DIM)
    uq_rope = jnp.pad(uq[:, :, HEAD_DIM:], ((0, 0), (0, 0), (0, LANES - MLA_ROPE_DIM)))
    w_uq2 = jnp.concatenate([uq[:, :, :HEAD_DIM].reshape(MLA_Q_RANK, WIDTH),
                             uq_rope.reshape(MLA_Q_RANK, WIDTH)], axis=1).astype(BF16)
    ukv = w_ukv[0].reshape(MLA_KV_RANK, HEADS, 2 * HEAD_DIM)
    w_uk = ukv[:, :, :HEAD_DIM].reshape(MLA_KV_RANK, WIDTH).astype(BF16)
    w_uvt = ukv[:, :, HEAD_DIM:].reshape(MLA_KV_RANK, WIDTH).T.astype(BF16)

    cos_a, sin_a, cos_b, sin_b = _rope_tables(pos, _rope_freqs())

    g_attn = row(attn_norm[0])
    qk_a, vt_a, kmean = _qkv_proj(x2, g_attn, w_qk, w_vt, cos_a, sin_a, FLASH_TK)
    qb, kn_b, vt_b, krx = _mla_proj(x2, g_attn, w_c, row(q_norm[0]), w_uq2, row(kv_norm[0]),
                                    w_uk, w_uvt, cos_b, sin_b, FLASH_TK)

    kmean_pad = jnp.pad(kmean.reshape(N_BLOCKS, WIDTH), ((0, LANES - N_BLOCKS), (0, 0)))
    bias = _moba_select(qk_a, kmean_pad)
    blk = jnp.arange(SEQ, dtype=jnp.int32)[:, None] // MOBA_BLOCK
    onehot = (blk == jnp.arange(LANES, dtype=jnp.int32)[None, :]).astype(BF16)

    y_a = _flash(qk_a, 0, bias, 0, qk_a, HEADS, onehot, vt_a, "flash_moba", FLASH_TQ, FLASH_TK)
    y_b = _flash(qb, 0, qb, HEADS, kn_b, 0, krx, vt_b, "flash_mla", FLASH_TQ, FLASH_TK)

    mixed = _gated_merge(x2, g_attn, w_g, y_a, y_b,
                         w_branch_a[0].astype(BF16), w_branch_b[0].astype(BF16))
    h = _out_proj(mixed, w_out[0].astype(BF16), x2)
    out = _ffn(h, row(ffn_norm[0]), w_gate[0].astype(BF16), w_up[0].astype(BF16),
               w_down[0].astype(BF16), row(final_norm))
    return out.reshape(1, SEQ, D_MODEL)
```

```python
import functools

import jax
import jax.numpy as jnp
from jax import lax
from jax.experimental import pallas as pl
from jax.experimental.pallas import tpu as pltpu

F32 = jnp.float32
BF16 = jnp.bfloat16

D_MODEL = 2048
SEQ = 16384
HEADS = 8
HEAD_DIM = 128
WIDTH = HEADS * HEAD_DIM
MOBA_BLOCK = 256
MOBA_TOPK = 3
N_BLOCKS = SEQ // MOBA_BLOCK
MOBA_ROPE_HALF = 16
MLA_ROPE_DIM = 64
MLA_ROPE_HALF = 32
MLA_Q_RANK = 512
MLA_KV_RANK = 256
ROPE_THETA = 500000.0
D_FF = 5632
EPS = 1e-6
NEG = -1e30
LANES = 128
SUBLANES = 8
FLASH_TQ = 1024
FLASH_TK = 512
FLASH_CHUNK = 64
MERGE_CHUNK = 512

LOG2E = 1.4426950408889634
MOBA_SCALE = HEAD_DIM ** -0.5 * LOG2E
MLA_SCALE = (HEAD_DIM + MLA_ROPE_DIM) ** -0.5 * LOG2E

VMEM_LIMIT = 52 * 1024 * 1024


def _params(*sem, flags=None):
    return pltpu.CompilerParams(dimension_semantics=sem, vmem_limit_bytes=VMEM_LIMIT, flags=flags)


def _rmsnorm_bf16(x, g):
    ms = jnp.mean(x * x, axis=-1, keepdims=True)
    return ((x * lax.rsqrt(ms + EPS)) * g).astype(BF16)


def _rope(x, cos, sin, half):
    lane = lax.broadcasted_iota(jnp.int32, x.shape, 1)
    partner = jnp.where(lane < half, pltpu.roll(x, LANES - half, 1), pltpu.roll(x, half, 1))
    return x * cos + partner * sin


def _rope_table_kernel(pos_ref, f_ref, ca_ref, sa_ref, cb_ref, sb_ref):
    ang = pos_ref[...].astype(F32) * f_ref[...]
    c = jnp.cos(ang)
    s = jnp.sin(ang)
    lane = lax.broadcasted_iota(jnp.int32, c.shape, 1)
    ha, hb = MOBA_ROPE_HALF, MLA_ROPE_HALF
    ca_ref[...] = jnp.where(lane < 2 * ha, c, 1.0)
    sa_ref[...] = jnp.where(lane < ha, -s, jnp.where(lane < 2 * ha, s, 0.0))
    c2 = pltpu.roll(c, LANES - 2 * ha, 1)
    s2 = pltpu.roll(s, LANES - 2 * ha, 1)
    cb_ref[...] = jnp.where(lane < 2 * hb, c2, 1.0)
    sb_ref[...] = jnp.where(lane < hb, -s2, jnp.where(lane < 2 * hb, s2, 0.0))


def _rope_tables(pos, freqs, tm=2048):
    out = jax.ShapeDtypeStruct((SEQ, LANES), F32)
    row = pl.BlockSpec((tm, LANES), lambda i: (i, 0))
    return pl.pallas_call(
        _rope_table_kernel,
        grid=(SEQ // tm,),
        in_specs=[pl.BlockSpec((tm, 1), lambda i: (i, 0)),
                  pl.BlockSpec((1, LANES), lambda i: (0, 0))],
        out_specs=[row, row, row, row],
        out_shape=[out, out, out, out],
        compiler_params=_params("parallel"),
        name="rope_tables",
    )(pos, freqs)


_NT = (((1,), (1,)), ((), ()))


def _head_major(tm):
    return pl.BlockSpec((HEADS, tm, HEAD_DIM), lambda i, *_: (0, i, 0))


def _store_vt(vt_ref, vt):
    for h in range(HEADS):
        vt_ref[h, 0] = vt[h * HEAD_DIM:(h + 1) * HEAD_DIM, :].astype(BF16)


def _qkv_kernel(x_ref, g_ref, w_ref, wvt_ref, cos_ref, sin_ref,
                q_ref, k_ref, vt_ref, kmean_ref, xn_ref):
    j = pl.program_id(1)
    tm = x_ref.shape[0]

    @pl.when(j == 0)
    def _():
        xn_ref[...] = _rmsnorm_bf16(x_ref[...], g_ref[...])

    def roped_heads():
        acc = jnp.dot(xn_ref[...], w_ref[...], preferred_element_type=F32)
        cos, sin = cos_ref[...], sin_ref[...]
        for h in range(HEADS):
            sl = slice(h * HEAD_DIM, (h + 1) * HEAD_DIM)
            yield h, sl, _rope(acc[:, sl], cos, sin, MOBA_ROPE_HALF)

    @pl.when(j == 0)
    def _():
        for h, _, r in roped_heads():
            q_ref[h] = (r * MOBA_SCALE).astype(BF16)

    @pl.when(j == 1)
    def _():
        for h, sl, r in roped_heads():
            k_ref[h] = r.astype(BF16)
            for b in range(tm // MOBA_BLOCK):
                rows = r[b * MOBA_BLOCK:(b + 1) * MOBA_BLOCK]
                kmean_ref[0, b:b + 1, sl] = jnp.mean(rows, axis=0, keepdims=True)

    @pl.when(j == 2)
    def _():
        vt = lax.dot_general(wvt_ref[...], xn_ref[...], _NT, preferred_element_type=F32)
        _store_vt(vt_ref, vt)


def _qkv_proj(x, g, w_qk, w_vt, cos_a, sin_a, tm):
    nb = tm // MOBA_BLOCK
    return pl.pallas_call(
        _qkv_kernel,
        grid=(SEQ // tm, 3),
        in_specs=[pl.BlockSpec((tm, D_MODEL), lambda i, j: (i, 0)),
                  pl.BlockSpec((1, D_MODEL), lambda i, j: (0, 0)),
                  pl.BlockSpec((D_MODEL, WIDTH), lambda i, j: (0, jnp.minimum(j, 1))),
                  pl.BlockSpec((WIDTH, D_MODEL), lambda i, j: (0, 0)),
                  pl.BlockSpec((tm, LANES), lambda i, j: (i, 0)),
                  pl.BlockSpec((tm, LANES), lambda i, j: (i, 0))],
        out_specs=[_head_major(tm), _head_major(tm),
                   pl.BlockSpec((HEADS, 1, HEAD_DIM, tm), lambda i, j: (0, i, 0, 0)),
                   pl.BlockSpec((1, nb, WIDTH), lambda i, j: (i, 0, 0))],
        out_shape=[jax.ShapeDtypeStruct((HEADS, SEQ, HEAD_DIM), BF16),
                   jax.ShapeDtypeStruct((HEADS, SEQ, HEAD_DIM), BF16),
                   jax.ShapeDtypeStruct((HEADS, SEQ // tm, HEAD_DIM, tm), BF16),
                   jax.ShapeDtypeStruct((SEQ // tm, nb, WIDTH), F32)],
        scratch_shapes=[pltpu.VMEM((tm, D_MODEL), BF16)],
        compiler_params=_params("parallel", "arbitrary"),
        name="qkv_proj",
    )(x, g, w_qk, w_vt, cos_a, sin_a)


def _mla_kernel(x_ref, g_ref, wc_ref, qg_ref, wuq_ref, kvg_ref, wuk_ref, wuvt_ref, cos_ref, sin_ref,
                qn_ref, qr_ref, kn_ref, vt_ref, kr_ref):
    xn = _rmsnorm_bf16(x_ref[...], g_ref[...])
    c = jnp.dot(xn, wc_ref[...], preferred_element_type=F32)
    cos, sin = cos_ref[...], sin_ref[...]
    kr = c[:, MLA_Q_RANK + MLA_KV_RANK:]
    kr_ref[...] = _rope(kr, cos, sin, MLA_ROPE_HALF).astype(BF16)

    cq = _rmsnorm_bf16(c[:, :MLA_Q_RANK], qg_ref[...])
    q = jnp.dot(cq, wuq_ref[...], preferred_element_type=F32)
    ckv = _rmsnorm_bf16(c[:, MLA_Q_RANK:MLA_Q_RANK + MLA_KV_RANK], kvg_ref[...])
    kn = jnp.dot(ckv, wuk_ref[...], preferred_element_type=F32)
    for h in range(HEADS):
        sl = slice(h * HEAD_DIM, (h + 1) * HEAD_DIM)
        sl_r = slice(WIDTH + h * HEAD_DIM, WIDTH + (h + 1) * HEAD_DIM)
        qn_ref[h] = (q[:, sl] * MLA_SCALE).astype(BF16)
        qr_ref[h] = (_rope(q[:, sl_r], cos, sin, MLA_ROPE_HALF) * MLA_SCALE).astype(BF16)
        kn_ref[h] = kn[:, sl].astype(BF16)
    _store_vt(vt_ref, lax.dot_general(wuvt_ref[...], ckv, _NT, preferred_element_type=F32))


def _mla_proj(x, g, w_c, qg, w_uq, kvg, w_uk, w_uvt, cos_b, sin_b, tm):
    nc = w_c.shape[1]
    full = lambda shape: pl.BlockSpec(shape, lambda i: (0, 0))
    return pl.pallas_call(
        _mla_kernel,
        grid=(SEQ // tm,),
        in_specs=[pl.BlockSpec((tm, D_MODEL), lambda i: (i, 0)),
                  full((1, D_MODEL)), full((D_MODEL, nc)),
                  full((1, MLA_Q_RANK)), full((MLA_Q_RANK, 2 * WIDTH)),
                  full((1, MLA_KV_RANK)), full((MLA_KV_RANK, WIDTH)), full((WIDTH, MLA_KV_RANK)),
                  pl.BlockSpec((tm, LANES), lambda i: (i, 0)),
                  pl.BlockSpec((tm, LANES), lambda i: (i, 0))],
        out_specs=[_head_major(tm), _head_major(tm), _head_major(tm),
                   pl.BlockSpec((HEADS, 1, HEAD_DIM, tm), lambda i: (0, i, 0, 0)),
                   pl.BlockSpec((tm, LANES), lambda i: (i, 0))],
        out_shape=[jax.ShapeDtypeStruct((HEADS, SEQ, HEAD_DIM), BF16)] * 3
        + [jax.ShapeDtypeStruct((HEADS, SEQ // tm, HEAD_DIM, tm), BF16),
                   jax.ShapeDtypeStruct((SEQ, LANES), BF16)],
        compiler_params=_params("parallel"),
        name="mla_proj",
    )(x, g, w_c, qg, w_uq, kvg, w_uk, w_uvt, cos_b, sin_b)


def _select_kernel(q_ref, kmean_ref, bias_ref):
    q = q_ref[0].astype(F32)
    gate = lax.dot_general(kmean_ref[...], q, _NT, preferred_element_type=F32,
                           precision=lax.Precision.HIGHEST)
    blk = lax.broadcasted_iota(jnp.int32, gate.shape, 0)
    qry = lax.broadcasted_iota(jnp.int32, gate.shape, 1)
    own = (pl.program_id(1) * gate.shape[1] + qry) // MOBA_BLOCK
    past = blk < own
    g = jnp.where(past, gate, -jnp.inf)
    sel = blk == own
    for _ in range(MOBA_TOPK):
        mx = jnp.max(g, axis=0, keepdims=True)
        cand = jnp.where((g == mx) & past, blk, LANES)
        idx = jnp.min(cand, axis=0, keepdims=True)
        hit = blk == idx
        sel = sel | hit
        g = jnp.where(hit, -jnp.inf, g)
    bias_t = jnp.where(sel, 0.0, jnp.where(blk < N_BLOCKS, NEG, 0.0))
    bias_ref[0] = jnp.transpose(bias_t).astype(BF16)


def _per_head(rows):
    return pl.BlockSpec((1, rows, HEAD_DIM), lambda h, i: (h, i, 0))


def _moba_select(q, kmean_pad, ts=2048):
    return pl.pallas_call(
        _select_kernel,
        grid=(HEADS, SEQ // ts),
        in_specs=[_per_head(ts),
                  pl.BlockSpec((LANES, HEAD_DIM), lambda h, i: (0, h))],
        out_specs=_per_head(ts),
        out_shape=jax.ShapeDtypeStruct((HEADS, SEQ, LANES), BF16),
        compiler_params=_params("parallel", "parallel"),
        name="moba_select",
    )(q, kmean_pad)


def _flash_kernel(q_ref, qx_ref, k_ref, kx_ref, vt_ref, o_ref,
                  s0_ref, s1_ref, p0_ref, p1_ref, acc_ref,
                  m_ref, max0_ref, max1_ref, alpha0_ref, alpha1_ref, *, tq, tk):
    i = pl.program_id(1)
    diag = tq // tk
    n_tiles = (i + 1) * diag
    qa = jnp.concatenate([q_ref[0], qx_ref[0]], axis=1)

    def tile_of(n):
        return jnp.where(n < diag, i * diag + n, n - diag)

    def scores(n):
        rows = pl.ds(pl.multiple_of(tile_of(n) * tk, tk), tk)
        ka = jnp.concatenate([k_ref[0, rows, :], kx_ref[rows, :]], axis=1)
        return lax.dot_general(ka, qa, _NT, preferred_element_type=F32)

    def diag_scores(n):
        s = scores(n)
        key = lax.broadcasted_iota(jnp.int32, s.shape, 0) + n * tk
        qry = lax.broadcasted_iota(jnp.int32, s.shape, 1)
        return jnp.where(key <= qry, s, NEG)

    chunks = [slice(c * FLASH_CHUNK, (c + 1) * FLASH_CHUNK) for c in range(tk // FLASH_CHUNK)]

    def fold8(x, op):
        out = x[0:SUBLANES]
        for r in range(1, x.shape[0] // SUBLANES):
            out = op(out, x[r * SUBLANES:(r + 1) * SUBLANES])
        return out

    def rep(x8, rows):
        return jnp.concatenate([x8] * (rows // SUBLANES), axis=0)

    s_slot, p_slot = (s0_ref, s1_ref), (p0_ref, p1_ref)
    max_slot, alpha_slot = (max0_ref, max1_ref), (alpha0_ref, alpha1_ref)
    ones_rows = jnp.ones((acc_ref.shape[0] - HEAD_DIM, tk), BF16)

    def score_stage(n, slot, masked=False):
        val = diag_scores(n) if masked else scores(n)
        s_slot[slot][...] = val
        mx = fold8(val[chunks[0], :], jnp.maximum)
        for c in chunks[1:]:
            mx = jnp.maximum(mx, fold8(val[c, :], jnp.maximum))
        max_slot[slot][...] = jnp.broadcast_to(jnp.max(mx, axis=0, keepdims=True), mx.shape)

    def exp_stage(slot):
        m = m_ref[...]
        m_new = jnp.maximum(m, max_slot[slot][...])
        m_ref[...] = m_new
        alpha_slot[slot][...] = jnp.exp2(m - m_new)
        m_rep = rep(m_new, FLASH_CHUNK)
        for c in chunks:
            shifted = s_slot[slot][c, :] - m_rep
            p_slot[slot][c, :] = jnp.exp2(shifted.astype(BF16))

    def value_stage(n, slot):
        alpha = alpha_slot[slot][...]
        vt1 =jnp.concatenate([vt_ref[0, tile_of(n)], ones_rows], axis=0)
        pv = jnp.dot(vt1, p_slot[slot][...], preferred_element_type=F32)
        acc_ref[...] = rep(alpha, acc_ref.shape[0]) * acc_ref[...] + pv

    assert diag == 2
    score_stage(0, 0, masked=True)
    score_stage(1, 1, masked=True)
    acc_ref[...] = jnp.zeros_like(acc_ref)
    m_ref[...] = jnp.full((SUBLANES, tq), NEG, F32)
    exp_stage(0)

    def step(k, a, last=False):
        if not last:
            score_stage(k + 2, a)
        exp_stage(1 - a)
        value_stage(k, a)

    def pair(kk, carry):
        step(2 * kk, 0)
        step(2 * kk + 1, 1)
        return carry

    lax.fori_loop(0, i, pair, 0)
    step(n_tiles - 2, 0, last=True)
    value_stage(n_tiles - 1, 1)
    acc = acc_ref[...]
    out_t = acc[:HEAD_DIM] / acc[HEAD_DIM:HEAD_DIM + 1]
    o_ref[0] = jnp.transpose(out_t).astype(o_ref.dtype)


def _flash(q_arr, qx_arr, k_arr, kx_arr, vt_arr, name, tq, tk):
    return pl.pallas_call(
        functools.partial(_flash_kernel, tq=tq, tk=tk),
        grid=(HEADS, SEQ // tq),
        in_specs=[_per_head(tq), _per_head(tq),
                  pl.BlockSpec((1, SEQ, HEAD_DIM), lambda h, i: (h, 0, 0)),
                  pl.BlockSpec((SEQ, LANES), lambda h, i: (0, 0)),
                  pl.BlockSpec((1, SEQ // tk, HEAD_DIM, tk), lambda h, i: (h, 0, 0, 0))],
        out_specs=_per_head(tq),
        out_shape=jax.ShapeDtypeStruct((HEADS, SEQ, HEAD_DIM), BF16),
        scratch_shapes=[pltpu.VMEM((tk, tq), F32), pltpu.VMEM((tk, tq), F32),
                        pltpu.VMEM((tk, tq), BF16), pltpu.VMEM((tk, tq), BF16),
                        pltpu.VMEM((HEAD_DIM + 2 * SUBLANES, tq), F32)]
        + [pltpu.VMEM((SUBLANES, tq), F32)] * 5,
        compiler_params=_params("parallel", "arbitrary"),
        name=name,
    )(q_arr, qx_arr, k_arr, kx_arr, vt_arr)


def _resident(shape):
    return pl.BlockSpec(shape, lambda i: (0,) * len(shape), pipeline_mode=pl.Buffered(1))


def _merge_kernel(x_ref, g_ref, wg_ref, ya_ref, yb_ref, wa_ref, wb_ref, o_ref):
    xn = _rmsnorm_bf16(x_ref[...], g_ref[...])
    ya = jnp.concatenate([ya_ref[h] for h in range(HEADS)], axis=1)
    yb = jnp.concatenate([yb_ref[h] for h in range(HEADS)], axis=1)
    for c in range(D_MODEL // MERGE_CHUNK):
        cols = slice(c * MERGE_CHUNK, (c + 1) * MERGE_CHUNK)
        cols_b = slice(D_MODEL + c * MERGE_CHUNK, D_MODEL + (c + 1) * MERGE_CHUNK)
        gate_a = jax.nn.sigmoid(jnp.dot(xn, wg_ref[:, cols], preferred_element_type=F32))
        gate_b = jax.nn.sigmoid(jnp.dot(xn, wg_ref[:, cols_b], preferred_element_type=F32))
        a = jnp.dot(ya, wa_ref[:, cols], preferred_element_type=F32)
        b = jnp.dot(yb, wb_ref[:, cols], preferred_element_type=F32)
        o_ref[:, cols] = (gate_a * a + gate_b * b).astype(BF16)


def _gated_merge(x, g, w_g, ya, yb, wa, wb, tm=512):
    return pl.pallas_call(
        _merge_kernel,
        grid=(SEQ // tm,),
        in_specs=[pl.BlockSpec((tm, D_MODEL), lambda i: (i, 0)),
                  _resident((1, D_MODEL)), _resident((D_MODEL, 2 * D_MODEL)),
                  _head_major(tm), _head_major(tm),
                  _resident((WIDTH, D_MODEL)), _resident((WIDTH, D_MODEL))],
        out_specs=pl.BlockSpec((tm, D_MODEL), lambda i: (i, 0)),
        out_shape=jax.ShapeDtypeStruct((SEQ, D_MODEL), BF16),
        compiler_params=_params("parallel"),
        name="gated_merge",
    )(x, g, w_g, ya, yb, wa, wb)


def _out_kernel(m_ref, w_ref, x_ref, o_ref):
    o_ref[...] = x_ref[...] + jnp.dot(m_ref[...], w_ref[...], preferred_element_type=F32)


def _out_proj(mixed, w_out, x, tm=512):
    return pl.pallas_call(
        _out_kernel,
        grid=(SEQ // tm,),
        in_specs=[pl.BlockSpec((tm, D_MODEL), lambda i: (i, 0)),
                  _resident((D_MODEL, D_MODEL)),
                  pl.BlockSpec((tm, D_MODEL), lambda i: (i, 0))],
        out_specs=pl.BlockSpec((tm, D_MODEL), lambda i: (i, 0)),
        out_shape=jax.ShapeDtypeStruct((SEQ, D_MODEL), F32),
        compiler_params=_params("parallel"),
        name="out_proj",
    )(mixed, w_out, x)


def _ffn_kernel(h_ref, g_ref, wg_ref, wu_ref, wd_ref, fg_ref, o_ref, hn_ref, acc_ref):
    f = pl.program_id(1)

    @pl.when(f == 0)
    def _():
        hn_ref[...] = _rmsnorm_bf16(h_ref[...], g_ref[...])
        acc_ref[...] = jnp.zeros_like(acc_ref)

    hn = hn_ref[...]
    gate = jnp.dot(hn, wg_ref[...], preferred_element_type=F32)
    up = jnp.dot(hn, wu_ref[...], preferred_element_type=F32)
    act = (jax.nn.silu(gate) * up).astype(BF16)
    acc_ref[...] += jnp.dot(act, wd_ref[...], preferred_element_type=F32)

    @pl.when(f == pl.num_programs(1) - 1)
    def _():
        y = h_ref[...] + acc_ref[...]
        ms = jnp.mean(y * y, axis=-1, keepdims=True)
        o_ref[...] = (y * lax.rsqrt(ms + EPS)) * fg_ref[...]


def _ffn(h, g, wg, wu, wd, fg, tm=512, tf=512):
    return pl.pallas_call(
        _ffn_kernel,
        grid=(SEQ // tm, D_FF // tf),
        in_specs=[pl.BlockSpec((tm, D_MODEL), lambda i, f: (i, 0)),
                  pl.BlockSpec((1, D_MODEL), lambda i, f: (0, 0)),
                  pl.BlockSpec((D_MODEL, tf), lambda i, f: (0, f)),
                  pl.BlockSpec((D_MODEL, tf), lambda i, f: (0, f)),
                  pl.BlockSpec((tf, D_MODEL), lambda i, f: (f, 0)),
                  pl.BlockSpec((1, D_MODEL), lambda i, f: (0, 0))],
        out_specs=pl.BlockSpec((tm, D_MODEL), lambda i, f: (i, 0)),
        out_shape=jax.ShapeDtypeStruct((SEQ, D_MODEL), F32),
        scratch_shapes=[pltpu.VMEM((tm, D_MODEL), BF16),
                        pltpu.VMEM((tm, D_MODEL), F32)],
        compiler_params=_params("parallel", "arbitrary"),
        name="ffn",
    )(h, g, wg, wu, wd, fg)


def _rope_freqs():
    def inv_freq(d):
        return ROPE_THETA ** (-jnp.arange(0, d, 2, dtype=F32) / d)

    fa = inv_freq(2 * MOBA_ROPE_HALF)
    fb = inv_freq(2 * MLA_ROPE_HALF)
    pad = jnp.zeros((LANES - 2 * MOBA_ROPE_HALF - 2 * MLA_ROPE_HALF,), F32)
    return jnp.concatenate([fa, fa, fb, fb, pad]).reshape(1, LANES)


def kernel(x, positions, attn_norm, w_in, q_norm, w_uq, kv_norm, w_ukv, w_branch_a, w_branch_b,
           w_out, ffn_norm, w_gate, w_up, w_down, final_norm):
    assert x.shape == (1, SEQ, D_MODEL) and w_in.shape[0] == 1
    x2 = x.reshape(SEQ, D_MODEL)
    pos = positions.reshape(SEQ, 1)
    row = lambda v: v.reshape(1, -1).astype(F32)

    w = w_in[0]
    c0 = 3 * WIDTH
    c1 = c0 + MLA_Q_RANK + MLA_KV_RANK + MLA_ROPE_DIM
    w_qk = w[:, :2 * WIDTH].astype(BF16)
    w_vt = w[:, 2 * WIDTH:c0].T.astype(BF16)
    w_c = jnp.pad(w[:, c0:c1], ((0, 0), (0, LANES - MLA_ROPE_DIM))).astype(BF16)
    w_g = w[:, c1:].astype(BF16)
    uq = w_uq[0].reshape(MLA_Q_RANK, HEADS, HEAD_DIM + MLA_ROPE_DIM)
    uq_rope = jnp.pad(uq[:, :, HEAD_DIM:], ((0, 0), (0, 0), (0, LANES - MLA_ROPE_DIM)))
    w_uq2 = jnp.concatenate([uq[:, :, :HEAD_DIM].reshape(MLA_Q_RANK, WIDTH),
                             uq_rope.reshape(MLA_Q_RANK, WIDTH)], axis=1).astype(BF16)
    ukv = w_ukv[0].reshape(MLA_KV_RANK, HEADS, 2 * HEAD_DIM)
    w_uk = ukv[:, :, :HEAD_DIM].reshape(MLA_KV_RANK, WIDTH).astype(BF16)
    w_uvt = ukv[:, :, HEAD_DIM:].reshape(MLA_KV_RANK, WIDTH).T.astype(BF16)

    cos_a, sin_a, cos_b, sin_b = _rope_tables(pos, _rope_freqs())

    g_attn = row(attn_norm[0])
    q_a, k_a, vt_a, kmean = _qkv_proj(x2, g_attn, w_qk, w_vt, cos_a, sin_a, FLASH_TK)
    qn_b, qr_b, kn_b, vt_b, krx = _mla_proj(x2, g_attn, w_c, row(q_norm[0]), w_uq2,
                                            row(kv_norm[0]), w_uk, w_uvt, cos_b, sin_b, FLASH_TK)

    kmean_pad = jnp.pad(kmean.reshape(N_BLOCKS, WIDTH), ((0, LANES - N_BLOCKS), (0, 0)))
    bias = _moba_select(q_a, kmean_pad)
    blk = jnp.arange(SEQ, dtype=jnp.int32)[:, None] // MOBA_BLOCK
    onehot = (blk == jnp.arange(LANES, dtype=jnp.int32)[None, :]).astype(BF16)

    y_a = _flash(q_a, bias, k_a, onehot, vt_a, "flash_moba", FLASH_TQ, FLASH_TK)
    y_b = _flash(qn_b, qr_b, kn_b, krx, vt_b, "flash_mla", FLASH_TQ, FLASH_TK)

    mixed = _gated_merge(x2, g_attn, w_g, y_a, y_b,
                         w_branch_a[0].astype(BF16), w_branch_b[0].astype(BF16))
    h = _out_proj(mixed, w_out[0].astype(BF16), x2)
    out = _ffn(h, row(ffn_norm[0]), w_gate[0].astype(BF16), w_up[0].astype(BF16),
               w_down[0].astype(BF16), row(final_norm))
    return out.reshape(1, SEQ, D_MODEL)
```

```python
import functools

import jax
import jax.numpy as jnp
from jax import lax
from jax.experimental import pallas as pl
from jax.experimental.pallas import tpu as pltpu

F32 = jnp.float32
BF16 = jnp.bfloat16

D_MODEL = 2048
SEQ = 16384
HEADS = 8
HEAD_DIM = 128
WIDTH = HEADS * HEAD_DIM
MOBA_BLOCK = 256
MOBA_TOPK = 3
N_BLOCKS = SEQ // MOBA_BLOCK
MOBA_ROPE_HALF = 16
MLA_ROPE_DIM = 64
MLA_ROPE_HALF = 32
MLA_Q_RANK = 512
MLA_KV_RANK = 256
ROPE_THETA = 500000.0
D_FF = 5632
EPS = 1e-6
NEG = -1e30
LANES = 128
SUBLANES = 8
FLASH_TQ = 1024
FLASH_TK = 512
FLASH_CHUNK = 64
FLASH_GROUP = 2
MERGE_CHUNK = 512

LOG2E = 1.4426950408889634
MOBA_SCALE = HEAD_DIM ** -0.5 * LOG2E
MLA_SCALE = (HEAD_DIM + MLA_ROPE_DIM) ** -0.5 * LOG2E

VMEM_LIMIT = 52 * 1024 * 1024


def _params(*sem, flags=None):
    return pltpu.CompilerParams(dimension_semantics=sem, vmem_limit_bytes=VMEM_LIMIT, flags=flags)


def _rmsnorm_bf16(x, g):
    ms = jnp.mean(x * x, axis=-1, keepdims=True)
    return ((x * lax.rsqrt(ms + EPS)) * g).astype(BF16)


def _rope(x, cos, sin, half):
    lane = lax.broadcasted_iota(jnp.int32, x.shape, 1)
    partner = jnp.where(lane < half, pltpu.roll(x, LANES - half, 1), pltpu.roll(x, half, 1))
    return x * cos + partner * sin


def _rope_table_kernel(pos_ref, f_ref, ca_ref, sa_ref, cb_ref, sb_ref):
    ang = pos_ref[...].astype(F32) * f_ref[...]
    c = jnp.cos(ang)
    s = jnp.sin(ang)
    lane = lax.broadcasted_iota(jnp.int32, c.shape, 1)
    ha, hb = MOBA_ROPE_HALF, MLA_ROPE_HALF
    ca_ref[...] = jnp.where(lane < 2 * ha, c, 1.0)
    sa_ref[...] = jnp.where(lane < ha, -s, jnp.where(lane < 2 * ha, s, 0.0))
    c2 = pltpu.roll(c, LANES - 2 * ha, 1)
    s2 = pltpu.roll(s, LANES - 2 * ha, 1)
    cb_ref[...] = jnp.where(lane < 2 * hb, c2, 1.0)
    sb_ref[...] = jnp.where(lane < hb, -s2, jnp.where(lane < 2 * hb, s2, 0.0))


def _rope_tables(pos, freqs, tm=2048):
    out = jax.ShapeDtypeStruct((SEQ, LANES), F32)
    row = pl.BlockSpec((tm, LANES), lambda i: (i, 0))
    return pl.pallas_call(
        _rope_table_kernel,
        grid=(SEQ // tm,),
        in_specs=[pl.BlockSpec((tm, 1), lambda i: (i, 0)),
                  pl.BlockSpec((1, LANES), lambda i: (0, 0))],
        out_specs=[row, row, row, row],
        out_shape=[out, out, out, out],
        compiler_params=_params("parallel"),
        name="rope_tables",
    )(pos, freqs)


_NT = (((1,), (1,)), ((), ()))


def _head_major(tm):
    return pl.BlockSpec((HEADS, tm, HEAD_DIM), lambda i, *_: (0, i, 0))


def _store_vt(vt_ref, vt):
    for h in range(HEADS):
        vt_ref[h, 0] = vt[h * HEAD_DIM:(h + 1) * HEAD_DIM, :].astype(BF16)


def _qkv_kernel(x_ref, g_ref, w_ref, wvt_ref, cos_ref, sin_ref,
                q_ref, k_ref, vt_ref, kmean_ref, xn_ref):
    j = pl.program_id(1)
    tm = x_ref.shape[0]

    @pl.when(j == 0)
    def _():
        xn_ref[...] = _rmsnorm_bf16(x_ref[...], g_ref[...])

    def roped_heads():
        acc = jnp.dot(xn_ref[...], w_ref[...], preferred_element_type=F32)
        cos, sin = cos_ref[...], sin_ref[...]
        for h in range(HEADS):
            sl = slice(h * HEAD_DIM, (h + 1) * HEAD_DIM)
            yield h, sl, _rope(acc[:, sl], cos, sin, MOBA_ROPE_HALF)

    @pl.when(j == 0)
    def _():
        for h, _, r in roped_heads():
            q_ref[h] = (r * MOBA_SCALE).astype(BF16)

    @pl.when(j == 1)
    def _():
        for h, sl, r in roped_heads():
            k_ref[h] = r.astype(BF16)
            for b in range(tm // MOBA_BLOCK):
                rows = r[b * MOBA_BLOCK:(b + 1) * MOBA_BLOCK]
                kmean_ref[0, b:b + 1, sl] = jnp.mean(rows, axis=0, keepdims=True)

    @pl.when(j == 2)
    def _():
        vt = lax.dot_general(wvt_ref[...], xn_ref[...], _NT, preferred_element_type=F32)
        _store_vt(vt_ref, vt)


def _qkv_proj(x, g, w_qk, w_vt, cos_a, sin_a, tm):
    nb = tm // MOBA_BLOCK
    return pl.pallas_call(
        _qkv_kernel,
        grid=(SEQ // tm, 3),
        in_specs=[pl.BlockSpec((tm, D_MODEL), lambda i, j: (i, 0)),
                  pl.BlockSpec((1, D_MODEL), lambda i, j: (0, 0)),
                  pl.BlockSpec((D_MODEL, WIDTH), lambda i, j: (0, jnp.minimum(j, 1))),
                  pl.BlockSpec((WIDTH, D_MODEL), lambda i, j: (0, 0)),
                  pl.BlockSpec((tm, LANES), lambda i, j: (i, 0)),
                  pl.BlockSpec((tm, LANES), lambda i, j: (i, 0))],
        out_specs=[_head_major(tm), _head_major(tm),
                   pl.BlockSpec((HEADS, 1, HEAD_DIM, tm), lambda i, j: (0, i, 0, 0)),
                   pl.BlockSpec((1, nb, WIDTH), lambda i, j: (i, 0, 0))],
        out_shape=[jax.ShapeDtypeStruct((HEADS, SEQ, HEAD_DIM), BF16),
                   jax.ShapeDtypeStruct((HEADS, SEQ, HEAD_DIM), BF16),
                   jax.ShapeDtypeStruct((HEADS, SEQ // tm, HEAD_DIM, tm), BF16),
                   jax.ShapeDtypeStruct((SEQ // tm, nb, WIDTH), F32)],
        scratch_shapes=[pltpu.VMEM((tm, D_MODEL), BF16)],
        compiler_params=_params("parallel", "arbitrary"),
        name="qkv_proj",
    )(x, g, w_qk, w_vt, cos_a, sin_a)


def _mla_kernel(x_ref, g_ref, wc_ref, qg_ref, wuq_ref, kvg_ref, wuk_ref, wuvt_ref, cos_ref, sin_ref,
                qn_ref, qr_ref, kn_ref, vt_ref, kr_ref):
    xn = _rmsnorm_bf16(x_ref[...], g_ref[...])
    c = jnp.dot(xn, wc_ref[...], preferred_element_type=F32)
    cos, sin = cos_ref[...], sin_ref[...]
    kr = c[:, MLA_Q_RANK + MLA_KV_RANK:]
    kr_ref[...] = _rope(kr, cos, sin, MLA_ROPE_HALF).astype(BF16)

    cq = _rmsnorm_bf16(c[:, :MLA_Q_RANK], qg_ref[...])
    q = jnp.dot(cq, wuq_ref[...], preferred_element_type=F32)
    ckv = _rmsnorm_bf16(c[:, MLA_Q_RANK:MLA_Q_RANK + MLA_KV_RANK], kvg_ref[...])
    kn = jnp.dot(ckv, wuk_ref[...], preferred_element_type=F32)
    for h in range(HEADS):
        sl = slice(h * HEAD_DIM, (h + 1) * HEAD_DIM)
        sl_r = slice(WIDTH + h * HEAD_DIM, WIDTH + (h + 1) * HEAD_DIM)
        qn_ref[h] = (q[:, sl] * MLA_SCALE).astype(BF16)
        qr_ref[h] = (_rope(q[:, sl_r], cos, sin, MLA_ROPE_HALF) * MLA_SCALE).astype(BF16)
        kn_ref[h] = kn[:, sl].astype(BF16)
    _store_vt(vt_ref, lax.dot_general(wuvt_ref[...], ckv, _NT, preferred_element_type=F32))


def _mla_proj(x, g, w_c, qg, w_uq, kvg, w_uk, w_uvt, cos_b, sin_b, tm):
    nc = w_c.shape[1]
    full = lambda shape: pl.BlockSpec(shape, lambda i: (0, 0))
    return pl.pallas_call(
        _mla_kernel,
        grid=(SEQ // tm,),
        in_specs=[pl.BlockSpec((tm, D_MODEL), lambda i: (i, 0)),
                  full((1, D_MODEL)), full((D_MODEL, nc)),
                  full((1, MLA_Q_RANK)), full((MLA_Q_RANK, 2 * WIDTH)),
                  full((1, MLA_KV_RANK)), full((MLA_KV_RANK, WIDTH)), full((WIDTH, MLA_KV_RANK)),
                  pl.BlockSpec((tm, LANES), lambda i: (i, 0)),
                  pl.BlockSpec((tm, LANES), lambda i: (i, 0))],
        out_specs=[_head_major(tm), _head_major(tm), _head_major(tm),
                   pl.BlockSpec((HEADS, 1, HEAD_DIM, tm), lambda i: (0, i, 0, 0)),
                   pl.BlockSpec((tm, LANES), lambda i: (i, 0))],
        out_shape=[jax.ShapeDtypeStruct((HEADS, SEQ, HEAD_DIM), BF16)] * 3
        + [jax.ShapeDtypeStruct((HEADS, SEQ // tm, HEAD_DIM, tm), BF16),
                   jax.ShapeDtypeStruct((SEQ, LANES), BF16)],
        compiler_params=_params("parallel"),
        name="mla_proj",
    )(x, g, w_c, qg, w_uq, kvg, w_uk, w_uvt, cos_b, sin_b)


def _select_kernel(q_ref, kmean_ref, bias_ref):
    q = q_ref[0].astype(F32)
    gate = lax.dot_general(kmean_ref[...], q, _NT, preferred_element_type=F32,
                           precision=lax.Precision.HIGHEST)
    blk = lax.broadcasted_iota(jnp.int32, gate.shape, 0)
    qry = lax.broadcasted_iota(jnp.int32, gate.shape, 1)
    own = (pl.program_id(1) * gate.shape[1] + qry) // MOBA_BLOCK
    past = blk < own
    g = jnp.where(past, gate, -jnp.inf)
    sel = blk == own
    for _ in range(MOBA_TOPK):
        mx = jnp.max(g, axis=0, keepdims=True)
        cand = jnp.where((g == mx) & past, blk, LANES)
        idx = jnp.min(cand, axis=0, keepdims=True)
        hit = blk == idx
        sel = sel | hit
        g = jnp.where(hit, -jnp.inf, g)
    bias_t = jnp.where(sel, 0.0, jnp.where(blk < N_BLOCKS, NEG, 0.0))
    bias_ref[0] = jnp.transpose(bias_t).astype(BF16)


def _per_head(rows):
    return pl.BlockSpec((1, rows, HEAD_DIM), lambda h, i: (h, i, 0))


def _moba_select(q, kmean_pad, ts=2048):
    return pl.pallas_call(
        _select_kernel,
        grid=(HEADS, SEQ // ts),
        in_specs=[_per_head(ts),
                  pl.BlockSpec((LANES, HEAD_DIM), lambda h, i: (0, h))],
        out_specs=_per_head(ts),
        out_shape=jax.ShapeDtypeStruct((HEADS, SEQ, LANES), BF16),
        compiler_params=_params("parallel", "parallel"),
        name="moba_select",
    )(q, kmean_pad)


FLASH_STREAM_SCRATCH = 10


def _flash_kernel(q_ref, qx_ref, k_ref, kx_ref, vt_ref, o_ref, *scratch, tq, tk):
    n_streams = q_ref.shape[0]
    streams = [_flash_stream(q_ref.at[h], qx_ref.at[h], k_ref.at[h], kx_ref, vt_ref.at[h],
                             o_ref.at[h], *scratch[h * FLASH_STREAM_SCRATCH:
                                                   (h + 1) * FLASH_STREAM_SCRATCH],
                             tq=tq, tk=tk)
               for h in range(n_streams)]
    i = pl.program_id(1)
    n_tiles = (i + 1) * (tq // tk)

    def on_all(stage, *args, **kwargs):
        for stream in streams:
            getattr(stream, stage)(*args, **kwargs)

    on_all("score_stage", 0, 0, masked=True)
    on_all("score_stage", 1, 1, masked=True)
    on_all("init")
    on_all("exp_stage", 0)

    def step(k, a, last=False):
        if not last:
            on_all("score_stage", k + 2, a)
        on_all("exp_stage", 1 - a)
        on_all("value_stage", k, a)

    def pair(kk, carry):
        step(2 * kk, 0)
        step(2 * kk + 1, 1)
        return carry

    lax.fori_loop(0, i, pair, 0)
    step(n_tiles - 2, 0, last=True)
    on_all("value_stage", n_tiles - 1, 1)
    on_all("finish")


class _FlashStream:
    pass


def _flash_stream(q_ref, qx_ref, k_ref, kx_ref, vt_ref, o_ref,
                  s0_ref, s1_ref, p0_ref, p1_ref, acc_ref,
                  m_ref, max0_ref, max1_ref, alpha0_ref, alpha1_ref, *, tq, tk):
    i = pl.program_id(1)
    diag = tq // tk
    assert diag == 2
    qa = jnp.concatenate([q_ref[...], qx_ref[...]], axis=1)

    def tile_of(n):
        return jnp.where(n < diag, i * diag + n, n - diag)

    def scores(n):
        rows = pl.ds(pl.multiple_of(tile_of(n) * tk, tk), tk)
        ka = jnp.concatenate([k_ref[rows, :], kx_ref[rows, :]], axis=1)
        return lax.dot_general(ka, qa, _NT, preferred_element_type=F32)

    def diag_scores(n):
        s = scores(n)
        key = lax.broadcasted_iota(jnp.int32, s.shape, 0) + n * tk
        qry = lax.broadcasted_iota(jnp.int32, s.shape, 1)
        return jnp.where(key <= qry, s, NEG)

    chunks = [slice(c * FLASH_CHUNK, (c + 1) * FLASH_CHUNK) for c in range(tk // FLASH_CHUNK)]

    def fold8(x, op):
        out = x[0:SUBLANES]
        for r in range(1, x.shape[0] // SUBLANES):
            out = op(out, x[r * SUBLANES:(r + 1) * SUBLANES])
        return out

    def rep(x8, rows):
        return jnp.concatenate([x8] * (rows // SUBLANES), axis=0)

    s_slot, p_slot = (s0_ref, s1_ref), (p0_ref, p1_ref)
    max_slot, alpha_slot = (max0_ref, max1_ref), (alpha0_ref, alpha1_ref)
    ones_rows = jnp.ones((acc_ref.shape[0] - HEAD_DIM, tk), BF16)

    def score_stage(n, slot, masked=False):
        val = diag_scores(n) if masked else scores(n)
        s_slot[slot][...] = val
        mx = fold8(val[chunks[0], :], jnp.maximum)
        for c in chunks[1:]:
            mx = jnp.maximum(mx, fold8(val[c, :], jnp.maximum))
        max_slot[slot][...] = jnp.broadcast_to(jnp.max(mx, axis=0, keepdims=True), mx.shape)

    def exp_stage(slot):
        m = m_ref[...]
        m_new = jnp.maximum(m, max_slot[slot][...])
        m_ref[...] = m_new
        alpha_slot[slot][...] = jnp.exp2(m - m_new)
        m_rep = rep(m_new, FLASH_CHUNK)
        for c in chunks:
            shifted = s_slot[slot][c, :] - m_rep
            p_slot[slot][c, :] = jnp.exp2(shifted.astype(BF16))

    def value_stage(n, slot):
        alpha = rep(alpha_slot[slot][...], acc_ref.shape[0])
        vt1 = jnp.concatenate([vt_ref[tile_of(n)], ones_rows], axis=0)
        pv = jnp.dot(vt1, p_slot[slot][...], preferred_element_type=F32)
        acc_ref[...] = alpha * acc_ref[...] + pv

    def init():
        acc_ref[...] = jnp.zeros_like(acc_ref)
        m_ref[...] = jnp.full((SUBLANES, tq), NEG, F32)

    def finish():
        acc = acc_ref[...]
        out_t = acc[:HEAD_DIM] / acc[HEAD_DIM:HEAD_DIM + 1]
        o_ref[...] = jnp.transpose(out_t).astype(o_ref.dtype)

    stream = _FlashStream()
    stream.score_stage, stream.exp_stage, stream.value_stage = score_stage, exp_stage, value_stage
    stream.init, stream.finish = init, finish
    return stream


def _flash(q_arr, qx_arr, k_arr, kx_arr, vt_arr, name, tq, tk, group):
    per_group = lambda rows: pl.BlockSpec((group, rows, HEAD_DIM), lambda g, i: (g, i, 0))
    whole = lambda shape: pl.BlockSpec(shape, lambda g, i: (g,) + (0,) * (len(shape) - 1),
                                       pipeline_mode=pl.Buffered(1))
    stream_scratch = (
        [pltpu.VMEM((tk, tq), F32)] * 2
        + [pltpu.VMEM((tk, tq), BF16)] * 2
        + [pltpu.VMEM((HEAD_DIM + 2 * SUBLANES, tq), F32)]
        + [pltpu.VMEM((SUBLANES, tq), F32)] * 5)
    assert len(stream_scratch) == FLASH_STREAM_SCRATCH
    return pl.pallas_call(
        functools.partial(_flash_kernel, tq=tq, tk=tk),
        grid=(HEADS // group, SEQ // tq),
        in_specs=[per_group(tq), per_group(tq),
                  whole((group, SEQ, HEAD_DIM)),
                  pl.BlockSpec((SEQ, LANES), lambda g, i: (0, 0), pipeline_mode=pl.Buffered(1)),
                  whole((group, SEQ // tk, HEAD_DIM, tk))],
        out_specs=per_group(tq),
        out_shape=jax.ShapeDtypeStruct((HEADS, SEQ, HEAD_DIM), BF16),
        scratch_shapes=stream_scratch * group,
        compiler_params=_params("parallel", "arbitrary"),
        name=name,
    )(q_arr, qx_arr, k_arr, kx_arr, vt_arr)


def _resident(shape):
    return pl.BlockSpec(shape, lambda i: (0,) * len(shape), pipeline_mode=pl.Buffered(1))


def _merge_kernel(x_ref, g_ref, wg_ref, ya_ref, yb_ref, wa_ref, wb_ref, o_ref):
    xn = _rmsnorm_bf16(x_ref[...], g_ref[...])
    ya = jnp.concatenate([ya_ref[h] for h in range(HEADS)], axis=1)
    yb = jnp.concatenate([yb_ref[h] for h in range(HEADS)], axis=1)
    for c in range(D_MODEL // MERGE_CHUNK):
        cols = slice(c * MERGE_CHUNK, (c + 1) * MERGE_CHUNK)
        cols_b = slice(D_MODEL + c * MERGE_CHUNK, D_MODEL + (c + 1) * MERGE_CHUNK)
        gate_a = jax.nn.sigmoid(jnp.dot(xn, wg_ref[:, cols], preferred_element_type=F32))
        gate_b = jax.nn.sigmoid(jnp.dot(xn, wg_ref[:, cols_b], preferred_element_type=F32))
        a = jnp.dot(ya, wa_ref[:, cols], preferred_element_type=F32)
        b = jnp.dot(yb, wb_ref[:, cols], preferred_element_type=F32)
        o_ref[:, cols] = (gate_a * a + gate_b * b).astype(BF16)


def _gated_merge(x, g, w_g, ya, yb, wa, wb, tm=512):
    return pl.pallas_call(
        _merge_kernel,
        grid=(SEQ // tm,),
        in_specs=[pl.BlockSpec((tm, D_MODEL), lambda i: (i, 0)),
                  _resident((1, D_MODEL)), _resident((D_MODEL, 2 * D_MODEL)),
                  _head_major(tm), _head_major(tm),
                  _resident((WIDTH, D_MODEL)), _resident((WIDTH, D_MODEL))],
        out_specs=pl.BlockSpec((tm, D_MODEL), lambda i: (i, 0)),
        out_shape=jax.ShapeDtypeStruct((SEQ, D_MODEL), BF16),
        compiler_params=_params("parallel"),
        name="gated_merge",
    )(x, g, w_g, ya, yb, wa, wb)


def _out_kernel(m_ref, w_ref, x_ref, o_ref):
    o_ref[...] = x_ref[...] + jnp.dot(m_ref[...], w_ref[...], preferred_element_type=F32)


def _out_proj(mixed, w_out, x, tm=512):
    return pl.pallas_call(
        _out_kernel,
        grid=(SEQ // tm,),
        in_specs=[pl.BlockSpec((tm, D_MODEL), lambda i: (i, 0)),
                  _resident((D_MODEL, D_MODEL)),
                  pl.BlockSpec((tm, D_MODEL), lambda i: (i, 0))],
        out_specs=pl.BlockSpec((tm, D_MODEL), lambda i: (i, 0)),
        out_shape=jax.ShapeDtypeStruct((SEQ, D_MODEL), F32),
        compiler_params=_params("parallel"),
        name="out_proj",
    )(mixed, w_out, x)


def _ffn_kernel(h_ref, g_ref, wg_ref, wu_ref, wd_ref, fg_ref, o_ref, hn_ref, acc_ref):
    f = pl.program_id(1)

    @pl.when(f == 0)
    def _():
        hn_ref[...] = _rmsnorm_bf16(h_ref[...], g_ref[...])
        acc_ref[...] = jnp.zeros_like(acc_ref)

    hn = hn_ref[...]
    gate = jnp.dot(hn, wg_ref[...], preferred_element_type=F32)
    up = jnp.dot(hn, wu_ref[...], preferred_element_type=F32)
    act = (jax.nn.silu(gate) * up).astype(BF16)
    acc_ref[...] += jnp.dot(act, wd_ref[...], preferred_element_type=F32)

    @pl.when(f == pl.num_programs(1) - 1)
    def _():
        y = h_ref[...] + acc_ref[...]
        ms = jnp.mean(y * y, axis=-1, keepdims=True)
        o_ref[...] = (y * lax.rsqrt(ms + EPS)) * fg_ref[...]


def _ffn(h, g, wg, wu, wd, fg, tm=512, tf=512):
    return pl.pallas_call(
        _ffn_kernel,
        grid=(SEQ // tm, D_FF // tf),
        in_specs=[pl.BlockSpec((tm, D_MODEL), lambda i, f: (i, 0)),
                  pl.BlockSpec((1, D_MODEL), lambda i, f: (0, 0)),
                  pl.BlockSpec((D_MODEL, tf), lambda i, f: (0, f)),
                  pl.BlockSpec((D_MODEL, tf), lambda i, f: (0, f)),
                  pl.BlockSpec((tf, D_MODEL), lambda i, f: (f, 0)),
                  pl.BlockSpec((1, D_MODEL), lambda i, f: (0, 0))],
        out_specs=pl.BlockSpec((tm, D_MODEL), lambda i, f: (i, 0)),
        out_shape=jax.ShapeDtypeStruct((SEQ, D_MODEL), F32),
        scratch_shapes=[pltpu.VMEM((tm, D_MODEL), BF16),
                        pltpu.VMEM((tm, D_MODEL), F32)],
        compiler_params=_params("parallel", "arbitrary"),
        name="ffn",
    )(h, g, wg, wu, wd, fg)


def _rope_freqs():
    def inv_freq(d):
        return ROPE_THETA ** (-jnp.arange(0, d, 2, dtype=F32) / d)

    fa = inv_freq(2 * MOBA_ROPE_HALF)
    fb = inv_freq(2 * MLA_ROPE_HALF)
    pad = jnp.zeros((LANES - 2 * MOBA_ROPE_HALF - 2 * MLA_ROPE_HALF,), F32)
    return jnp.concatenate([fa, fa, fb, fb, pad]).reshape(1, LANES)


def kernel(x, positions, attn_norm, w_in, q_norm, w_uq, kv_norm, w_ukv, w_branch_a, w_branch_b,
           w_out, ffn_norm, w_gate, w_up, w_down, final_norm):
    assert x.shape == (1, SEQ, D_MODEL) and w_in.shape[0] == 1
    x2 = x.reshape(SEQ, D_MODEL)
    pos = positions.reshape(SEQ, 1)
    row = lambda v: v.reshape(1, -1).astype(F32)

    w = w_in[0]
    c0 = 3 * WIDTH
    c1 = c0 + MLA_Q_RANK + MLA_KV_RANK + MLA_ROPE_DIM
    w_qk = w[:, :2 * WIDTH].astype(BF16)
    w_vt = w[:, 2 * WIDTH:c0].T.astype(BF16)
    w_c = jnp.pad(w[:, c0:c1], ((0, 0), (0, LANES - MLA_ROPE_DIM))).astype(BF16)
    w_g = w[:, c1:].astype(BF16)
    uq = w_uq[0].reshape(MLA_Q_RANK, HEADS, HEAD_DIM + MLA_ROPE_DIM)
    uq_rope = jnp.pad(uq[:, :, HEAD_DIM:], ((0, 0), (0, 0), (0, LANES - MLA_ROPE_DIM)))
    w_uq2 = jnp.concatenate([uq[:, :, :HEAD_DIM].reshape(MLA_Q_RANK, WIDTH),
                             uq_rope.reshape(MLA_Q_RANK, WIDTH)], axis=1).astype(BF16)
    ukv = w_ukv[0].reshape(MLA_KV_RANK, HEADS, 2 * HEAD_DIM)
    w_uk = ukv[:, :, :HEAD_DIM].reshape(MLA_KV_RANK, WIDTH).astype(BF16)
    w_uvt = ukv[:, :, HEAD_DIM:].reshape(MLA_KV_RANK, WIDTH).T.astype(BF16)

    cos_a, sin_a, cos_b, sin_b = _rope_tables(pos, _rope_freqs())

    g_attn = row(attn_norm[0])
    q_a, k_a, vt_a, kmean = _qkv_proj(x2, g_attn, w_qk, w_vt, cos_a, sin_a, FLASH_TK)
    qn_b, qr_b, kn_b, vt_b, krx = _mla_proj(x2, g_attn, w_c, row(q_norm[0]), w_uq2,
                                            row(kv_norm[0]), w_uk, w_uvt, cos_b, sin_b, FLASH_TK)

    kmean_pad = jnp.pad(kmean.reshape(N_BLOCKS, WIDTH), ((0, LANES - N_BLOCKS), (0, 0)))
    bias = _moba_select(q_a, kmean_pad)
    blk = jnp.arange(SEQ, dtype=jnp.int32)[:, None] // MOBA_BLOCK
    onehot = (blk == jnp.arange(LANES, dtype=jnp.int32)[None, :]).astype(BF16)

    y_a = _flash(q_a, bias, k_a, onehot, vt_a, "flash_moba", FLASH_TQ, FLASH_TK, FLASH_GROUP)
    y_b = _flash(qn_b, qr_b, kn_b, krx, vt_b, "flash_mla", FLASH_TQ, FLASH_TK, FLASH_GROUP)

    mixed = _gated_merge(x2, g_attn, w_g, y_a, y_b,
                         w_branch_a[0].astype(BF16), w_branch_b[0].astype(BF16))
    h = _out_proj(mixed, w_out[0].astype(BF16), x2)
    out = _ffn(h, row(ffn_norm[0]), w_gate[0].astype(BF16), w_up[0].astype(BF16),
               w_down[0].astype(BF16), row(final_norm))
    return out.reshape(1, SEQ, D_MODEL)
```

```python
import functools

import jax
import jax.numpy as jnp
from jax import lax
from jax.experimental import pallas as pl
from jax.experimental.pallas import tpu as pltpu

F32 = jnp.float32
BF16 = jnp.bfloat16

D_MODEL = 2048
SEQ = 16384
HEADS = 8
HEAD_DIM = 128
WIDTH = HEADS * HEAD_DIM
MOBA_BLOCK = 256
MOBA_TOPK = 3
N_BLOCKS = SEQ // MOBA_BLOCK
MOBA_ROPE_HALF = 16
MLA_ROPE_DIM = 64
MLA_ROPE_HALF = 32
MLA_Q_RANK = 512
MLA_KV_RANK = 256
ROPE_THETA = 500000.0
D_FF = 5632
EPS = 1e-6
NEG = -1e30
LANES = 128
SUBLANES = 8
FLASH_TQ = 1024
FLASH_TK = 512
FLASH_CHUNK = 64
FLASH_GROUP = 2
MERGE_CHUNK = 512

LOG2E = 1.4426950408889634
MOBA_SCALE = HEAD_DIM ** -0.5 * LOG2E
MLA_SCALE = (HEAD_DIM + MLA_ROPE_DIM) ** -0.5 * LOG2E

VMEM_LIMIT = 52 * 1024 * 1024


def _params(*sem, flags=None):
    return pltpu.CompilerParams(dimension_semantics=sem, vmem_limit_bytes=VMEM_LIMIT, flags=flags)


def _rmsnorm_bf16(x, g):
    ms = jnp.mean(x * x, axis=-1, keepdims=True)
    return ((x * lax.rsqrt(ms + EPS)) * g).astype(BF16)


def _rope(x, cos, sin, half):
    lane = lax.broadcasted_iota(jnp.int32, x.shape, 1)
    partner = jnp.where(lane < half, pltpu.roll(x, LANES - half, 1), pltpu.roll(x, half, 1))
    return x * cos + partner * sin


def _rope_table_kernel(pos_ref, f_ref, ca_ref, sa_ref, cb_ref, sb_ref):
    ang = pos_ref[...].astype(F32) * f_ref[...]
    c = jnp.cos(ang)
    s = jnp.sin(ang)
    lane = lax.broadcasted_iota(jnp.int32, c.shape, 1)
    ha, hb = MOBA_ROPE_HALF, MLA_ROPE_HALF
    ca_ref[...] = jnp.where(lane < 2 * ha, c, 1.0)
    sa_ref[...] = jnp.where(lane < ha, -s, jnp.where(lane < 2 * ha, s, 0.0))
    c2 = pltpu.roll(c, LANES - 2 * ha, 1)
    s2 = pltpu.roll(s, LANES - 2 * ha, 1)
    cb_ref[...] = jnp.where(lane < 2 * hb, c2, 1.0)
    sb_ref[...] = jnp.where(lane < hb, -s2, jnp.where(lane < 2 * hb, s2, 0.0))


def _rope_tables(pos, freqs, tm=2048):
    out = jax.ShapeDtypeStruct((SEQ, LANES), F32)
    row = pl.BlockSpec((tm, LANES), lambda i: (i, 0))
    return pl.pallas_call(
        _rope_table_kernel,
        grid=(SEQ // tm,),
        in_specs=[pl.BlockSpec((tm, 1), lambda i: (i, 0)),
                  pl.BlockSpec((1, LANES), lambda i: (0, 0))],
        out_specs=[row, row, row, row],
        out_shape=[out, out, out, out],
        compiler_params=_params("parallel"),
        name="rope_tables",
    )(pos, freqs)


_NT = (((1,), (1,)), ((), ()))


def _head_major(tm):
    return pl.BlockSpec((HEADS, tm, HEAD_DIM), lambda i, *_: (0, i, 0))


def _store_vt(vt_ref, vt):
    for h in range(HEADS):
        vt_ref[h, 0] = vt[h * HEAD_DIM:(h + 1) * HEAD_DIM, :].astype(BF16)


def _resident(shape):
    return pl.BlockSpec(shape, lambda i: (0,) * len(shape), pipeline_mode=pl.Buffered(1))


def _inproj_kernel(x_ref, g_ref, wq_ref, wk_ref, wvt_ref, cosa_ref, sina_ref,
                   wc_ref, qg_ref, wuq_ref, kvg_ref, wuk_ref, wuvt_ref, cos_ref, sin_ref,
                   q_ref, k_ref, vta_ref, kmean_ref,
                   qn_ref, qr_ref, kn_ref, vt_ref, kr_ref):
    tm = x_ref.shape[0]
    xn = _rmsnorm_bf16(x_ref[...], g_ref[...])

    cos, sin = cosa_ref[...], sina_ref[...]
    q = jnp.dot(xn, wq_ref[...], preferred_element_type=F32)
    for h in range(HEADS):
        sl = slice(h * HEAD_DIM, (h + 1) * HEAD_DIM)
        q_ref[h] = (_rope(q[:, sl], cos, sin, MOBA_ROPE_HALF) * MOBA_SCALE).astype(BF16)
    k = jnp.dot(xn, wk_ref[...], preferred_element_type=F32)
    for h in range(HEADS):
        sl = slice(h * HEAD_DIM, (h + 1) * HEAD_DIM)
        r = _rope(k[:, sl], cos, sin, MOBA_ROPE_HALF)
        k_ref[h] = r.astype(BF16)
        for b in range(tm // MOBA_BLOCK):
            rows = r[b * MOBA_BLOCK:(b + 1) * MOBA_BLOCK]
            kmean_ref[0, b:b + 1, sl] = jnp.mean(rows, axis=0, keepdims=True)
    _store_vt(vta_ref, lax.dot_general(wvt_ref[...], xn, _NT, preferred_element_type=F32))

    c = jnp.dot(xn, wc_ref[...], preferred_element_type=F32)
    cos, sin = cos_ref[...], sin_ref[...]
    kr = c[:, MLA_Q_RANK + MLA_KV_RANK:]
    kr_ref[...] = _rope(kr, cos, sin, MLA_ROPE_HALF).astype(BF16)

    cq = _rmsnorm_bf16(c[:, :MLA_Q_RANK], qg_ref[...])
    q = jnp.dot(cq, wuq_ref[...], preferred_element_type=F32)
    ckv = _rmsnorm_bf16(c[:, MLA_Q_RANK:MLA_Q_RANK + MLA_KV_RANK], kvg_ref[...])
    kn = jnp.dot(ckv, wuk_ref[...], preferred_element_type=F32)
    for h in range(HEADS):
        sl = slice(h * HEAD_DIM, (h + 1) * HEAD_DIM)
        sl_r = slice(WIDTH + h * HEAD_DIM, WIDTH + (h + 1) * HEAD_DIM)
        qn_ref[h] = (q[:, sl] * MLA_SCALE).astype(BF16)
        qr_ref[h] = (_rope(q[:, sl_r], cos, sin, MLA_ROPE_HALF) * MLA_SCALE).astype(BF16)
        kn_ref[h] = kn[:, sl].astype(BF16)
    _store_vt(vt_ref, lax.dot_general(wuvt_ref[...], ckv, _NT, preferred_element_type=F32))


def _in_proj(x, g, w_q, w_k, w_vt, cos_a, sin_a, w_c, qg, w_uq, kvg, w_uk, w_uvt, cos_b, sin_b, tm):
    nb = tm // MOBA_BLOCK
    rows = lambda width: pl.BlockSpec((tm, width), lambda i: (i, 0))
    vt_spec = pl.BlockSpec((HEADS, 1, HEAD_DIM, tm), lambda i: (0, i, 0, 0))
    head_major = jax.ShapeDtypeStruct((HEADS, SEQ, HEAD_DIM), BF16)
    vt_shape = jax.ShapeDtypeStruct((HEADS, SEQ // tm, HEAD_DIM, tm), BF16)
    return pl.pallas_call(
        _inproj_kernel,
        grid=(SEQ // tm,),
        in_specs=[rows(D_MODEL), _resident((1, D_MODEL)),
                  _resident((D_MODEL, WIDTH)), _resident((D_MODEL, WIDTH)),
                  _resident((WIDTH, D_MODEL)), rows(LANES), rows(LANES),
                  _resident((D_MODEL, w_c.shape[1])),
                  _resident((1, MLA_Q_RANK)), _resident((MLA_Q_RANK, 2 * WIDTH)),
                  _resident((1, MLA_KV_RANK)), _resident((MLA_KV_RANK, WIDTH)),
                  _resident((WIDTH, MLA_KV_RANK)), rows(LANES), rows(LANES)],
        out_specs=[_head_major(tm), _head_major(tm), vt_spec,
                   pl.BlockSpec((1, nb, WIDTH), lambda i: (i, 0, 0)),
                   _head_major(tm), _head_major(tm), _head_major(tm), vt_spec, rows(LANES)],
        out_shape=[head_major, head_major, vt_shape,
                   jax.ShapeDtypeStruct((SEQ // tm, nb, WIDTH), F32),
                   head_major, head_major, head_major, vt_shape,
                   jax.ShapeDtypeStruct((SEQ, LANES), BF16)],
        compiler_params=_params("parallel"),
        name="in_proj",
    )(x, g, w_q, w_k, w_vt, cos_a, sin_a, w_c, qg, w_uq, kvg, w_uk, w_uvt, cos_b, sin_b)


def _select_kernel(q_ref, kmean_ref, bias_ref):
    q = q_ref[0].astype(F32)
    gate = lax.dot_general(kmean_ref[...], q, _NT, preferred_element_type=F32,
                           precision=lax.Precision.HIGHEST)
    blk = lax.broadcasted_iota(jnp.int32, gate.shape, 0)
    qry = lax.broadcasted_iota(jnp.int32, gate.shape, 1)
    own = (pl.program_id(1) * gate.shape[1] + qry) // MOBA_BLOCK
    past = blk < own
    g = jnp.where(past, gate, -jnp.inf)
    sel = blk == own
    for _ in range(MOBA_TOPK):
        mx = jnp.max(g, axis=0, keepdims=True)
        cand = jnp.where((g == mx) & past, blk, LANES)
        idx = jnp.min(cand, axis=0, keepdims=True)
        hit = blk == idx
        sel = sel | hit
        g = jnp.where(hit, -jnp.inf, g)
    bias_t = jnp.where(sel, 0.0, jnp.where(blk < N_BLOCKS, NEG, 0.0))
    bias_ref[0] = jnp.transpose(bias_t).astype(BF16)


def _per_head(rows):
    return pl.BlockSpec((1, rows, HEAD_DIM), lambda h, i: (h, i, 0))


def _moba_select(q, kmean_pad, ts=2048):
    return pl.pallas_call(
        _select_kernel,
        grid=(HEADS, SEQ // ts),
        in_specs=[_per_head(ts),
                  pl.BlockSpec((LANES, HEAD_DIM), lambda h, i: (0, h))],
        out_specs=_per_head(ts),
        out_shape=jax.ShapeDtypeStruct((HEADS, SEQ, LANES), BF16),
        compiler_params=_params("parallel", "parallel"),
        name="moba_select",
    )(q, kmean_pad)


FLASH_STREAM_SCRATCH = 10


def _flash_kernel(q_ref, qx_ref, k_ref, kx_ref, vt_ref, o_ref, *scratch, tq, tk):
    n_streams = q_ref.shape[0]
    streams = [_flash_stream(q_ref.at[h], qx_ref.at[h], k_ref.at[h], kx_ref, vt_ref.at[h],
                             o_ref.at[h], *scratch[h * FLASH_STREAM_SCRATCH:
                                                   (h + 1) * FLASH_STREAM_SCRATCH],
                             tq=tq, tk=tk)
               for h in range(n_streams)]
    i = pl.program_id(1)
    n_tiles = (i + 1) * (tq // tk)

    def on_all(stage, *args, **kwargs):
        for stream in streams:
            getattr(stream, stage)(*args, **kwargs)

    on_all("score_stage", 0, 0, masked=True)
    on_all("score_stage", 1, 1, masked=True)
    on_all("init")
    on_all("exp_stage", 0)

    def step(k, a, last=False):
        if not last:
            on_all("score_stage", k + 2, a)
        on_all("exp_stage", 1 - a)
        on_all("value_stage", k, a)

    def pair(kk, carry):
        step(2 * kk, 0)
        step(2 * kk + 1, 1)
        return carry

    lax.fori_loop(0, i, pair, 0)
    step(n_tiles - 2, 0, last=True)
    on_all("value_stage", n_tiles - 1, 1)
    on_all("finish")


class _FlashStream:
    pass


def _flash_stream(q_ref, qx_ref, k_ref, kx_ref, vt_ref, o_ref,
                  s0_ref, s1_ref, p0_ref, p1_ref, acc_ref,
                  m_ref, max0_ref, max1_ref, alpha0_ref, alpha1_ref, *, tq, tk):
    i = pl.program_id(1)
    diag = tq // tk
    assert diag == 2
    qa = jnp.concatenate([q_ref[...], qx_ref[...]], axis=1)

    def tile_of(n):
        return jnp.where(n < diag, i * diag + n, n - diag)

    def scores(n):
        rows = pl.ds(pl.multiple_of(tile_of(n) * tk, tk), tk)
        ka = jnp.concatenate([k_ref[rows, :], kx_ref[rows, :]], axis=1)
        return lax.dot_general(ka, qa, _NT, preferred_element_type=F32)

    def diag_scores(n):
        s = scores(n)
        key = lax.broadcasted_iota(jnp.int32, s.shape, 0) + n * tk
        qry = lax.broadcasted_iota(jnp.int32, s.shape, 1)
        return jnp.where(key <= qry, s, NEG)

    chunks = [slice(c * FLASH_CHUNK, (c + 1) * FLASH_CHUNK) for c in range(tk // FLASH_CHUNK)]

    def fold8(x, op):
        out = x[0:SUBLANES]
        for r in range(1, x.shape[0] // SUBLANES):
            out = op(out, x[r * SUBLANES:(r + 1) * SUBLANES])
        return out

    def rep(x8, rows):
        return jnp.concatenate([x8] * (rows // SUBLANES), axis=0)

    s_slot, p_slot = (s0_ref, s1_ref), (p0_ref, p1_ref)
    max_slot, alpha_slot = (max0_ref, max1_ref), (alpha0_ref, alpha1_ref)
    ones_rows = jnp.ones((acc_ref.shape[0] - HEAD_DIM, tk), BF16)

    def score_stage(n, slot, masked=False):
        val = diag_scores(n) if masked else scores(n)
        s_slot[slot][...] = val
        mx = fold8(val[chunks[0], :], jnp.maximum)
        for c in chunks[1:]:
            mx = jnp.maximum(mx, fold8(val[c, :], jnp.maximum))
        max_slot[slot][...] = jnp.broadcast_to(jnp.max(mx, axis=0, keepdims=True), mx.shape)

    def exp_stage(slot):
        m = m_ref[...]
        m_new = jnp.maximum(m, max_slot[slot][...])
        m_ref[...] = m_new
        alpha_slot[slot][...] = jnp.exp2(m - m_new)
        m_rep = rep(m_new, FLASH_CHUNK)
        for c in chunks:
            shifted = s_slot[slot][c, :] - m_rep
            p_slot[slot][c, :] = jnp.exp2(shifted.astype(BF16))

    def value_stage(n, slot):
        alpha = rep(alpha_slot[slot][...], acc_ref.shape[0])
        vt1 = jnp.concatenate([vt_ref[tile_of(n)], ones_rows], axis=0)
        pv = jnp.dot(vt1, p_slot[slot][...], preferred_element_type=F32)
        acc_ref[...] = alpha * acc_ref[...] + pv

    def init():
        acc_ref[...] = jnp.zeros_like(acc_ref)
        m_ref[...] = jnp.full((SUBLANES, tq), NEG, F32)

    def finish():
        acc = acc_ref[...]
        out_t = acc[:HEAD_DIM] / acc[HEAD_DIM:HEAD_DIM + 1]
        o_ref[...] = jnp.transpose(out_t).astype(o_ref.dtype)

    stream = _FlashStream()
    stream.score_stage, stream.exp_stage, stream.value_stage = score_stage, exp_stage, value_stage
    stream.init, stream.finish = init, finish
    return stream


def _flash(q_arr, qx_arr, k_arr, kx_arr, vt_arr, name, tq, tk, group):
    per_group = lambda rows: pl.BlockSpec((group, rows, HEAD_DIM), lambda g, i: (g, i, 0))
    whole = lambda shape: pl.BlockSpec(shape, lambda g, i: (g,) + (0,) * (len(shape) - 1),
                                       pipeline_mode=pl.Buffered(1))
    stream_scratch = (
        [pltpu.VMEM((tk, tq), F32)] * 2
        + [pltpu.VMEM((tk, tq), BF16)] * 2
        + [pltpu.VMEM((HEAD_DIM + 2 * SUBLANES, tq), F32)]
        + [pltpu.VMEM((SUBLANES, tq), F32)] * 5)
    assert len(stream_scratch) == FLASH_STREAM_SCRATCH
    return pl.pallas_call(
        functools.partial(_flash_kernel, tq=tq, tk=tk),
        grid=(HEADS // group, SEQ // tq),
        in_specs=[per_group(tq), per_group(tq),
                  whole((group, SEQ, HEAD_DIM)),
                  pl.BlockSpec((SEQ, LANES), lambda g, i: (0, 0), pipeline_mode=pl.Buffered(1)),
                  whole((group, SEQ // tk, HEAD_DIM, tk))],
        out_specs=per_group(tq),
        out_shape=jax.ShapeDtypeStruct((HEADS, SEQ, HEAD_DIM), BF16),
        scratch_shapes=stream_scratch * group,
        compiler_params=_params("parallel", "arbitrary"),
        name=name,
    )(q_arr, qx_arr, k_arr, kx_arr, vt_arr)


def _merge_kernel(x_ref, g_ref, wg_ref, ya_ref, yb_ref, wa_ref, wb_ref, o_ref):
    xn = _rmsnorm_bf16(x_ref[...], g_ref[...])
    ya = jnp.concatenate([ya_ref[h] for h in range(HEADS)], axis=1)
    yb = jnp.concatenate([yb_ref[h] for h in range(HEADS)], axis=1)
    for c in range(D_MODEL // MERGE_CHUNK):
        cols = slice(c * MERGE_CHUNK, (c + 1) * MERGE_CHUNK)
        cols_b = slice(D_MODEL + c * MERGE_CHUNK, D_MODEL + (c + 1) * MERGE_CHUNK)
        gate_a = jax.nn.sigmoid(jnp.dot(xn, wg_ref[:, cols], preferred_element_type=F32))
        gate_b = jax.nn.sigmoid(jnp.dot(xn, wg_ref[:, cols_b], preferred_element_type=F32))
        a = jnp.dot(ya, wa_ref[:, cols], preferred_element_type=F32)
        b = jnp.dot(yb, wb_ref[:, cols], preferred_element_type=F32)
        o_ref[:, cols] = (gate_a * a + gate_b * b).astype(BF16)


def _gated_merge(x, g, w_g, ya, yb, wa, wb, tm=512):
    return pl.pallas_call(
        _merge_kernel,
        grid=(SEQ // tm,),
        in_specs=[pl.BlockSpec((tm, D_MODEL), lambda i: (i, 0)),
                  _resident((1, D_MODEL)), _resident((D_MODEL, 2 * D_MODEL)),
                  _head_major(tm), _head_major(tm),
                  _resident((WIDTH, D_MODEL)), _resident((WIDTH, D_MODEL))],
        out_specs=pl.BlockSpec((tm, D_MODEL), lambda i: (i, 0)),
        out_shape=jax.ShapeDtypeStruct((SEQ, D_MODEL), BF16),
        compiler_params=_params("parallel"),
        name="gated_merge",
    )(x, g, w_g, ya, yb, wa, wb)


def _out_kernel(m_ref, w_ref, x_ref, o_ref):
    o_ref[...] = x_ref[...] + jnp.dot(m_ref[...], w_ref[...], preferred_element_type=F32)


def _out_proj(mixed, w_out, x, tm=512):
    return pl.pallas_call(
        _out_kernel,
        grid=(SEQ // tm,),
        in_specs=[pl.BlockSpec((tm, D_MODEL), lambda i: (i, 0)),
                  _resident((D_MODEL, D_MODEL)),
                  pl.BlockSpec((tm, D_MODEL), lambda i: (i, 0))],
        out_specs=pl.BlockSpec((tm, D_MODEL), lambda i: (i, 0)),
        out_shape=jax.ShapeDtypeStruct((SEQ, D_MODEL), F32),
        compiler_params=_params("parallel"),
        name="out_proj",
    )(mixed, w_out, x)


def _ffn_kernel(h_ref, g_ref, wg_ref, wu_ref, wd_ref, fg_ref, o_ref, hn_ref, acc_ref):
    f = pl.program_id(1)

    @pl.when(f == 0)
    def _():
        hn_ref[...] = _rmsnorm_bf16(h_ref[...], g_ref[...])
        acc_ref[...] = jnp.zeros_like(acc_ref)

    hn = hn_ref[...]
    gate = jnp.dot(hn, wg_ref[...], preferred_element_type=F32)
    up = jnp.dot(hn, wu_ref[...], preferred_element_type=F32)
    act = (jax.nn.silu(gate) * up).astype(BF16)
    acc_ref[...] += jnp.dot(act, wd_ref[...], preferred_element_type=F32)

    @pl.when(f == pl.num_programs(1) - 1)
    def _():
        y = h_ref[...] + acc_ref[...]
        ms = jnp.mean(y * y, axis=-1, keepdims=True)
        o_ref[...] = (y * lax.rsqrt(ms + EPS)) * fg_ref[...]


def _ffn(h, g, wg, wu, wd, fg, tm=512, tf=512):
    return pl.pallas_call(
        _ffn_kernel,
        grid=(SEQ // tm, D_FF // tf),
        in_specs=[pl.BlockSpec((tm, D_MODEL), lambda i, f: (i, 0)),
                  pl.BlockSpec((1, D_MODEL), lambda i, f: (0, 0)),
                  pl.BlockSpec((D_MODEL, tf), lambda i, f: (0, f)),
                  pl.BlockSpec((D_MODEL, tf), lambda i, f: (0, f)),
                  pl.BlockSpec((tf, D_MODEL), lambda i, f: (f, 0)),
                  pl.BlockSpec((1, D_MODEL), lambda i, f: (0, 0))],
        out_specs=pl.BlockSpec((tm, D_MODEL), lambda i, f: (i, 0)),
        out_shape=jax.ShapeDtypeStruct((SEQ, D_MODEL), F32),
        scratch_shapes=[pltpu.VMEM((tm, D_MODEL), BF16),
                        pltpu.VMEM((tm, D_MODEL), F32)],
        compiler_params=_params("parallel", "arbitrary"),
        name="ffn",
    )(h, g, wg, wu, wd, fg)


def _rope_freqs():
    def inv_freq(d):
        return ROPE_THETA ** (-jnp.arange(0, d, 2, dtype=F32) / d)

    fa = inv_freq(2 * MOBA_ROPE_HALF)
    fb = inv_freq(2 * MLA_ROPE_HALF)
    pad = jnp.zeros((LANES - 2 * MOBA_ROPE_HALF - 2 * MLA_ROPE_HALF,), F32)
    return jnp.concatenate([fa, fa, fb, fb, pad]).reshape(1, LANES)


def kernel(x, positions, attn_norm, w_in, q_norm, w_uq, kv_norm, w_ukv, w_branch_a, w_branch_b,
           w_out, ffn_norm, w_gate, w_up, w_down, final_norm):
    assert x.shape == (1, SEQ, D_MODEL) and w_in.shape[0] == 1
    x2 = x.reshape(SEQ, D_MODEL)
    pos = positions.reshape(SEQ, 1)
    row = lambda v: v.reshape(1, -1).astype(F32)

    w = w_in[0]
    c0 = 3 * WIDTH
    c1 = c0 + MLA_Q_RANK + MLA_KV_RANK + MLA_ROPE_DIM
    w_q = w[:, :WIDTH].astype(BF16)
    w_k = w[:, WIDTH:2 * WIDTH].astype(BF16)
    w_vt =w[:, 2 * WIDTH:c0].T.astype(BF16)
    w_c = jnp.pad(w[:, c0:c1], ((0, 0), (0, LANES - MLA_ROPE_DIM))).astype(BF16)
    w_g = w[:, c1:].astype(BF16)
    uq = w_uq[0].reshape(MLA_Q_RANK, HEADS, HEAD_DIM + MLA_ROPE_DIM)
    uq_rope = jnp.pad(uq[:, :, HEAD_DIM:], ((0, 0), (0, 0), (0, LANES - MLA_ROPE_DIM)))
    w_uq2 = jnp.concatenate([uq[:, :, :HEAD_DIM].reshape(MLA_Q_RANK, WIDTH),
                             uq_rope.reshape(MLA_Q_RANK, WIDTH)], axis=1).astype(BF16)
    ukv = w_ukv[0].reshape(MLA_KV_RANK, HEADS, 2 * HEAD_DIM)
    w_uk = ukv[:, :, :HEAD_DIM].reshape(MLA_KV_RANK, WIDTH).astype(BF16)
    w_uvt = ukv[:, :, HEAD_DIM:].reshape(MLA_KV_RANK, WIDTH).T.astype(BF16)

    cos_a, sin_a, cos_b, sin_b = _rope_tables(pos, _rope_freqs())

    g_attn = row(attn_norm[0])
    q_a, k_a, vt_a, kmean, qn_b, qr_b, kn_b, vt_b, krx = _in_proj(
        x2, g_attn, w_q, w_k, w_vt, cos_a, sin_a,
        w_c, row(q_norm[0]), w_uq2, row(kv_norm[0]), w_uk, w_uvt, cos_b, sin_b, FLASH_TK)

    kmean_pad = jnp.pad(kmean.reshape(N_BLOCKS, WIDTH), ((0, LANES - N_BLOCKS), (0, 0)))
    bias = _moba_select(q_a, kmean_pad)
    blk = jnp.arange(SEQ, dtype=jnp.int32)[:, None] // MOBA_BLOCK
    onehot = (blk == jnp.arange(LANES, dtype=jnp.int32)[None, :]).astype(BF16)

    y_a = _flash(q_a, bias, k_a, onehot, vt_a, "flash_moba", FLASH_TQ, FLASH_TK, FLASH_GROUP)
    y_b = _flash(qn_b, qr_b, kn_b, krx, vt_b, "flash_mla", FLASH_TQ, FLASH_TK, FLASH_GROUP)

    mixed = _gated_merge(x2, g_attn, w_g, y_a, y_b,
                         w_branch_a[0].astype(BF16), w_branch_b[0].astype(BF16))
    h = _out_proj(mixed, w_out[0].astype(BF16), x2)
    out = _ffn(h, row(ffn_norm[0]), w_gate[0].astype(BF16), w_up[0].astype(BF16),
               w_down[0].astype(BF16), row(final_norm))
    return out.reshape(1, SEQ, D_MODEL)
```

```python
import functools

import jax
import jax.numpy as jnp
from jax import lax
from jax.experimental import pallas as pl
from jax.experimental.pallas import tpu as pltpu

F32 = jnp.float32
BF16 = jnp.bfloat16

D_MODEL = 2048
SEQ = 16384
HEADS = 8
HEAD_DIM = 128
WIDTH = HEADS * HEAD_DIM
MOBA_BLOCK = 256
MOBA_TOPK = 3
N_BLOCKS = SEQ // MOBA_BLOCK
MOBA_ROPE_HALF = 16
MLA_ROPE_DIM = 64
MLA_ROPE_HALF = 32
MLA_Q_RANK = 512
MLA_KV_RANK = 256
ROPE_THETA = 500000.0
D_FF = 5632
EPS = 1e-6
NEG = -1e30
LANES = 128
SUBLANES = 8
FLASH_TQ = 1024
FLASH_TK = 512
FLASH_CHUNK = 64
FLASH_GROUP = 2
MERGE_CHUNK = 512

LOG2E = 1.4426950408889634
MOBA_SCALE = HEAD_DIM ** -0.5 * LOG2E
MLA_SCALE = (HEAD_DIM + MLA_ROPE_DIM) ** -0.5 * LOG2E

VMEM_LIMIT = 52 * 1024 * 1024


def _params(*sem, flags=None):
    return pltpu.CompilerParams(dimension_semantics=sem, vmem_limit_bytes=VMEM_LIMIT, flags=flags)


def _rmsnorm_bf16(x, g):
    ms = jnp.mean(x * x, axis=-1, keepdims=True)
    return ((x * lax.rsqrt(ms + EPS)) * g).astype(BF16)


def _rope(x, cos, sin, half):
    lane = lax.broadcasted_iota(jnp.int32, x.shape, 1)
    partner = jnp.where(lane < half, pltpu.roll(x, LANES - half, 1), pltpu.roll(x, half, 1))
    return x * cos + partner * sin


def _rope_tables(pos, freqs):
    ang = pos.astype(F32) * freqs
    c = jnp.cos(ang)
    s = jnp.sin(ang)
    lane = lax.broadcasted_iota(jnp.int32, c.shape, 1)
    ha, hb = MOBA_ROPE_HALF, MLA_ROPE_HALF
    cos_a = jnp.where(lane < 2 * ha, c, 1.0)
    sin_a = jnp.where(lane < ha, -s, jnp.where(lane < 2 * ha, s, 0.0))
    c2 = pltpu.roll(c, LANES - 2 * ha, 1)
    s2 = pltpu.roll(s, LANES - 2 * ha, 1)
    cos_b = jnp.where(lane < 2 * hb, c2, 1.0)
    sin_b = jnp.where(lane < hb, -s2, jnp.where(lane < 2 * hb, s2, 0.0))
    return cos_a, sin_a, cos_b, sin_b


def _moba_bias_t(gate, first_query):
    blk = lax.broadcasted_iota(jnp.int32, gate.shape, 0)
    qry = lax.broadcasted_iota(jnp.int32, gate.shape, 1)
    own = (first_query + qry) // MOBA_BLOCK
    past = blk < own
    g = jnp.where(past, gate, -jnp.inf)
    sel = blk == own
    for _ in range(MOBA_TOPK):
        mx = jnp.max(g, axis=0, keepdims=True)
        cand = jnp.where((g == mx) & past, blk, N_BLOCKS)
        idx = jnp.min(cand, axis=0, keepdims=True)
        hit = blk == idx
        sel = sel | hit
        g = jnp.where(hit, -jnp.inf, g)
    bias = jnp.where(sel, 0.0, NEG)
    return jnp.concatenate([bias, jnp.zeros((LANES - N_BLOCKS, gate.shape[1]), F32)], axis=0)


_NT = (((1,), (1,)), ((), ()))


def _head_major(tm):
    return pl.BlockSpec((HEADS, tm, HEAD_DIM), lambda i, *_: (0, i, 0))


def _store_vt(vt_ref, vt):
    for h in range(HEADS):
        vt_ref[h, 0] = vt[h * HEAD_DIM:(h + 1) * HEAD_DIM, :].astype(BF16)


def _resident(shape):
    return pl.BlockSpec(shape, lambda i: (0,) * len(shape), pipeline_mode=pl.Buffered(1))


def _inproj_kernel(x_ref, g_ref, pos_ref, freq_ref, wq_ref, wk_ref, wvt_ref,
                   wc_ref, qg_ref, wuq_ref, kvg_ref, wuk_ref, wuvt_ref,
                   q_ref, k_ref, vta_ref, bias_ref,
                   qn_ref, qr_ref, kn_ref, vt_ref, kr_ref, kmean_ref):
    i = pl.program_id(0)
    tm = x_ref.shape[0]
    xn = _rmsnorm_bf16(x_ref[...], g_ref[...])
    cos, sin, cos_b, sin_b = _rope_tables(pos_ref[...], freq_ref[...])

    @pl.when(i == 0)
    def _():
        kmean_ref[...] = jnp.zeros_like(kmean_ref)

    k = jnp.dot(xn, wk_ref[...], preferred_element_type=F32)
    slot = lax.broadcasted_iota(jnp.int32, (N_BLOCKS, HEAD_DIM), 0)
    for h in range(HEADS):
        sl = slice(h * HEAD_DIM, (h + 1) * HEAD_DIM)
        r = _rope(k[:, sl], cos, sin, MOBA_ROPE_HALF)
        k_ref[h] = r.astype(BF16)
        means = kmean_ref[:, sl]
        for b in range(tm // MOBA_BLOCK):
            rows = r[b * MOBA_BLOCK:(b + 1) * MOBA_BLOCK]
            means = jnp.where(slot == i * (tm // MOBA_BLOCK) + b,
                              jnp.mean(rows, axis=0, keepdims=True), means)
        kmean_ref[:, sl] = means
    q = jnp.dot(xn, wq_ref[...], preferred_element_type=F32)
    for h in range(HEADS):
        sl = slice(h * HEAD_DIM, (h + 1) * HEAD_DIM)
        qh = _rope(q[:, sl], cos, sin, MOBA_ROPE_HALF) * MOBA_SCALE
        q_ref[h] = qh.astype(BF16)
        gate = lax.dot_general(kmean_ref[:, sl], qh, _NT, preferred_element_type=F32,
                               precision=lax.Precision.HIGHEST)
        bias_ref[h] = jnp.transpose(_moba_bias_t(gate, i * tm)).astype(BF16)
    _store_vt(vta_ref, lax.dot_general(wvt_ref[...], xn, _NT, preferred_element_type=F32))

    c = jnp.dot(xn, wc_ref[...], preferred_element_type=F32)
    cos, sin = cos_b, sin_b
    kr = c[:, MLA_Q_RANK + MLA_KV_RANK:]
    kr_ref[...] = _rope(kr, cos, sin, MLA_ROPE_HALF).astype(BF16)

    cq = _rmsnorm_bf16(c[:, :MLA_Q_RANK], qg_ref[...])
    q = jnp.dot(cq, wuq_ref[...], preferred_element_type=F32)
    ckv = _rmsnorm_bf16(c[:, MLA_Q_RANK:MLA_Q_RANK + MLA_KV_RANK], kvg_ref[...])
    kn = jnp.dot(ckv, wuk_ref[...], preferred_element_type=F32)
    for h in range(HEADS):
        sl = slice(h * HEAD_DIM, (h + 1) * HEAD_DIM)
        sl_r = slice(WIDTH + h * HEAD_DIM, WIDTH + (h + 1) * HEAD_DIM)
        qn_ref[h] = (q[:, sl] * MLA_SCALE).astype(BF16)
        qr_ref[h] = (_rope(q[:, sl_r], cos, sin, MLA_ROPE_HALF) * MLA_SCALE).astype(BF16)
        kn_ref[h] = kn[:, sl].astype(BF16)
    _store_vt(vt_ref, lax.dot_general(wuvt_ref[...], ckv, _NT, preferred_element_type=F32))


def _in_proj(x, g, pos, freqs, w_q, w_k, w_vt, w_c, qg, w_uq, kvg, w_uk, w_uvt, tm):
    rows = lambda width: pl.BlockSpec((tm, width), lambda i: (i, 0))
    vt_spec = pl.BlockSpec((HEADS, 1, HEAD_DIM, tm), lambda i: (0, i, 0, 0))
    head_major = jax.ShapeDtypeStruct((HEADS, SEQ, HEAD_DIM), BF16)
    vt_shape = jax.ShapeDtypeStruct((HEADS, SEQ // tm, HEAD_DIM, tm), BF16)
    return pl.pallas_call(
        _inproj_kernel,
        grid=(SEQ // tm,),
        in_specs=[rows(D_MODEL), _resident((1, D_MODEL)), rows(1), _resident((1, LANES)),
                  _resident((D_MODEL, WIDTH)), _resident((D_MODEL, WIDTH)),
                  _resident((WIDTH, D_MODEL)),
                  _resident((D_MODEL, w_c.shape[1])),
                  _resident((1, MLA_Q_RANK)), _resident((MLA_Q_RANK, 2 * WIDTH)),
                  _resident((1, MLA_KV_RANK)), _resident((MLA_KV_RANK, WIDTH)),
                  _resident((WIDTH, MLA_KV_RANK))],
        out_specs=[_head_major(tm), _head_major(tm), vt_spec, _head_major(tm),
                   _head_major(tm), _head_major(tm), _head_major(tm), vt_spec, rows(LANES)],
        out_shape=[head_major, head_major, vt_shape, head_major,
                   head_major, head_major, head_major, vt_shape,
                   jax.ShapeDtypeStruct((SEQ, LANES), BF16)],
        scratch_shapes=[pltpu.VMEM((N_BLOCKS, WIDTH), F32)],
        compiler_params=_params("arbitrary"),
        name="in_proj",
    )(x, g, pos, freqs, w_q, w_k, w_vt, w_c, qg, w_uq, kvg, w_uk, w_uvt)


FLASH_STREAM_SCRATCH = 10


def _flash_kernel(q_ref, qx_ref, k_ref, kx_ref, vt_ref, o_ref, *scratch, tq, tk):
    n_streams = q_ref.shape[0]
    streams = [_flash_stream(q_ref.at[h], qx_ref.at[h], k_ref.at[h], kx_ref, vt_ref.at[h],
                             o_ref.at[h], *scratch[h * FLASH_STREAM_SCRATCH:
                                                   (h + 1) * FLASH_STREAM_SCRATCH],
                             tq=tq, tk=tk)
               for h in range(n_streams)]
    i = pl.program_id(1)
    n_tiles = (i + 1) * (tq // tk)

    def on_all(stage, *args, **kwargs):
        for stream in streams:
            getattr(stream, stage)(*args, **kwargs)

    on_all("score_stage", 0, 0, masked=True)
    on_all("score_stage", 1, 1, masked=True)
    on_all("init")
    on_all("exp_stage", 0)

    def step(k, a, last=False):
        if not last:
            on_all("score_stage", k + 2, a)
        on_all("exp_stage", 1 - a)
        on_all("value_stage", k, a)

    def pair(kk, carry):
        step(2 * kk, 0)
        step(2 * kk + 1, 1)
        return carry

    lax.fori_loop(0, i, pair, 0)
    step(n_tiles - 2, 0, last=True)
    on_all("value_stage", n_tiles - 1, 1)
    on_all("finish")


class _FlashStream:
    pass


def _flash_stream(q_ref, qx_ref, k_ref, kx_ref, vt_ref, o_ref,
                  s0_ref, s1_ref, p0_ref, p1_ref, acc_ref,
                  m_ref, max0_ref, max1_ref, alpha0_ref, alpha1_ref, *, tq, tk):
    i = pl.program_id(1)
    diag = tq // tk
    assert diag == 2
    qa = jnp.concatenate([q_ref[...], qx_ref[...]], axis=1)

    def tile_of(n):
        return jnp.where(n < diag, i * diag + n, n - diag)

    def scores(n):
        rows = pl.ds(pl.multiple_of(tile_of(n) * tk, tk), tk)
        ka = jnp.concatenate([k_ref[rows, :], kx_ref[rows, :]], axis=1)
        return lax.dot_general(ka, qa, _NT, preferred_element_type=F32)

    def diag_scores(n):
        s = scores(n)
        key = lax.broadcasted_iota(jnp.int32, s.shape, 0) + n * tk
        qry = lax.broadcasted_iota(jnp.int32, s.shape, 1)
        return jnp.where(key <= qry, s, NEG)

    chunks = [slice(c * FLASH_CHUNK, (c + 1) * FLASH_CHUNK) for c in range(tk // FLASH_CHUNK)]

    def fold8(x, op):
        out = x[0:SUBLANES]
        for r in range(1, x.shape[0] // SUBLANES):
            out = op(out, x[r * SUBLANES:(r + 1) * SUBLANES])
        return out

    def rep(x8, rows):
        return jnp.concatenate([x8] * (rows // SUBLANES), axis=0)

    s_slot, p_slot = (s0_ref, s1_ref), (p0_ref, p1_ref)
    max_slot, alpha_slot = (max0_ref, max1_ref), (alpha0_ref, alpha1_ref)
    ones_rows = jnp.ones((acc_ref.shape[0] - HEAD_DIM, tk), BF16)

    def score_stage(n, slot, masked=False):
        val = diag_scores(n) if masked else scores(n)
        s_slot[slot][...] = val
        mx = fold8(val[chunks[0], :], jnp.maximum)
        for c in chunks[1:]:
            mx = jnp.maximum(mx, fold8(val[c, :], jnp.maximum))
        max_slot[slot][...] = jnp.broadcast_to(jnp.max(mx, axis=0, keepdims=True), mx.shape)

    def exp_stage(slot):
        m = m_ref[...]
        m_new = jnp.maximum(m, max_slot[slot][...])
        m_ref[...] = m_new
        alpha_slot[slot][...] = jnp.exp2(m - m_new)
        m_rep = rep(m_new, FLASH_CHUNK)
        for c in chunks:
            shifted = s_slot[slot][c, :] - m_rep
            p_slot[slot][c, :] = jnp.exp2(shifted.astype(BF16))

    def value_stage(n, slot):
        alpha = rep(alpha_slot[slot][...], acc_ref.shape[0])
        vt1 = jnp.concatenate([vt_ref[tile_of(n)], ones_rows], axis=0)
        pv = jnp.dot(vt1, p_slot[slot][...], preferred_element_type=F32)
        acc_ref[...] = alpha * acc_ref[...] + pv

    def init():
        acc_ref[...] = jnp.zeros_like(acc_ref)
        m_ref[...] = jnp.full((SUBLANES, tq), NEG, F32)

    def finish():
        acc = acc_ref[...]
        out_t = acc[:HEAD_DIM] / acc[HEAD_DIM:HEAD_DIM + 1]
        o_ref[...] = jnp.transpose(out_t).astype(o_ref.dtype)

    stream = _FlashStream()
    stream.score_stage, stream.exp_stage, stream.value_stage = score_stage, exp_stage, value_stage
    stream.init, stream.finish = init, finish
    return stream


def _flash(q_arr, qx_arr, k_arr, kx_arr, vt_arr, name, tq, tk, group):
    per_group = lambda rows: pl.BlockSpec((group, rows, HEAD_DIM), lambda g, i: (g, i, 0))
    whole = lambda shape: pl.BlockSpec(shape, lambda g, i: (g,) + (0,) * (len(shape) - 1),
                                       pipeline_mode=pl.Buffered(1))
    stream_scratch = (
        [pltpu.VMEM((tk, tq), F32)] * 2
        + [pltpu.VMEM((tk, tq), BF16)] * 2
        + [pltpu.VMEM((HEAD_DIM + 2 * SUBLANES, tq), F32)]
        + [pltpu.VMEM((SUBLANES, tq), F32)] * 5)
    assert len(stream_scratch) == FLASH_STREAM_SCRATCH
    return pl.pallas_call(
        functools.partial(_flash_kernel, tq=tq, tk=tk),
        grid=(HEADS // group, SEQ // tq),
        in_specs=[per_group(tq), per_group(tq),
                  whole((group, SEQ, HEAD_DIM)),
                  pl.BlockSpec((SEQ, LANES), lambda g, i: (0, 0), pipeline_mode=pl.Buffered(1)),
                  whole((group, SEQ // tk, HEAD_DIM, tk))],
        out_specs=per_group(tq),
        out_shape=jax.ShapeDtypeStruct((HEADS, SEQ, HEAD_DIM), BF16),
        scratch_shapes=stream_scratch * group,
        compiler_params=_params("parallel", "arbitrary"),
        name=name,
    )(q_arr, qx_arr, k_arr, kx_arr, vt_arr)


def _merge_kernel(x_ref, g_ref, wg_ref, ya_ref, yb_ref, wa_ref, wb_ref, o_ref):
    xn = _rmsnorm_bf16(x_ref[...], g_ref[...])
    ya = jnp.concatenate([ya_ref[h] for h in range(HEADS)], axis=1)
    yb = jnp.concatenate([yb_ref[h] for h in range(HEADS)], axis=1)
    for c in range(D_MODEL // MERGE_CHUNK):
        cols = slice(c * MERGE_CHUNK, (c + 1) * MERGE_CHUNK)
        cols_b = slice(D_MODEL + c * MERGE_CHUNK, D_MODEL + (c + 1) * MERGE_CHUNK)
        gate_a = jax.nn.sigmoid(jnp.dot(xn, wg_ref[:, cols], preferred_element_type=F32))
        gate_b = jax.nn.sigmoid(jnp.dot(xn, wg_ref[:, cols_b], preferred_element_type=F32))
        a = jnp.dot(ya, wa_ref[:, cols], preferred_element_type=F32)
        b = jnp.dot(yb, wb_ref[:, cols], preferred_element_type=F32)
        o_ref[:, cols] = (gate_a * a + gate_b * b).astype(BF16)


def _gated_merge(x, g, w_g, ya, yb, wa, wb, tm=512):
    return pl.pallas_call(
        _merge_kernel,
        grid=(SEQ // tm,),
        in_specs=[pl.BlockSpec((tm, D_MODEL), lambda i: (i, 0)),
                  _resident((1, D_MODEL)), _resident((D_MODEL, 2 * D_MODEL)),
                  _head_major(tm), _head_major(tm),
                  _resident((WIDTH, D_MODEL)), _resident((WIDTH, D_MODEL))],
        out_specs=pl.BlockSpec((tm, D_MODEL), lambda i: (i, 0)),
        out_shape=jax.ShapeDtypeStruct((SEQ, D_MODEL), BF16),
        compiler_params=_params("parallel"),
        name="gated_merge",
    )(x, g, w_g, ya, yb, wa, wb)


def _out_kernel(m_ref, w_ref, x_ref, o_ref):
    o_ref[...] = x_ref[...] + jnp.dot(m_ref[...], w_ref[...], preferred_element_type=F32)


def _out_proj(mixed, w_out, x, tm=512):
    return pl.pallas_call(
        _out_kernel,
        grid=(SEQ // tm,),
        in_specs=[pl.BlockSpec((tm, D_MODEL), lambda i: (i, 0)),
                  _resident((D_MODEL, D_MODEL)),
                  pl.BlockSpec((tm, D_MODEL), lambda i: (i, 0))],
        out_specs=pl.BlockSpec((tm, D_MODEL), lambda i: (i, 0)),
        out_shape=jax.ShapeDtypeStruct((SEQ, D_MODEL), F32),
        compiler_params=_params("parallel"),
        name="out_proj",
    )(mixed, w_out, x)


def _ffn_kernel(h_ref, g_ref, wg_ref, wu_ref, wd_ref, fg_ref, o_ref, hn_ref, acc_ref):
    f = pl.program_id(1)

    @pl.when(f == 0)
    def _():
        hn_ref[...] = _rmsnorm_bf16(h_ref[...], g_ref[...])
        acc_ref[...] = jnp.zeros_like(acc_ref)

    hn = hn_ref[...]
    gate = jnp.dot(hn, wg_ref[...], preferred_element_type=F32)
    up = jnp.dot(hn, wu_ref[...], preferred_element_type=F32)
    act = (jax.nn.silu(gate) * up).astype(BF16)
    acc_ref[...] += jnp.dot(act, wd_ref[...], preferred_element_type=F32)

    @pl.when(f == pl.num_programs(1) - 1)
    def _():
        y = h_ref[...] + acc_ref[...]
        ms = jnp.mean(y * y, axis=-1, keepdims=True)
        o_ref[...] = (y * lax.rsqrt(ms + EPS)) * fg_ref[...]


def _ffn(h, g, wg, wu, wd, fg, tm=512, tf=512):
    return pl.pallas_call(
        _ffn_kernel,
        grid=(SEQ // tm, D_FF // tf),
        in_specs=[pl.BlockSpec((tm, D_MODEL), lambda i, f: (i, 0)),
                  pl.BlockSpec((1, D_MODEL), lambda i, f: (0, 0)),
                  pl.BlockSpec((D_MODEL, tf), lambda i, f: (0, f)),
                  pl.BlockSpec((D_MODEL, tf), lambda i, f: (0, f)),
                  pl.BlockSpec((tf, D_MODEL), lambda i, f: (f, 0)),
                  pl.BlockSpec((1, D_MODEL), lambda i, f: (0, 0))],
        out_specs=pl.BlockSpec((tm, D_MODEL), lambda i, f: (i, 0)),
        out_shape=jax.ShapeDtypeStruct((SEQ, D_MODEL), F32),
        scratch_shapes=[pltpu.VMEM((tm, D_MODEL), BF16),
                        pltpu.VMEM((tm, D_MODEL), F32)],
        compiler_params=_params("parallel", "arbitrary"),
        name="ffn",
    )(h, g, wg, wu, wd, fg)


def _rope_freqs():
    def inv_freq(d):
        return ROPE_THETA ** (-jnp.arange(0, d, 2, dtype=F32) / d)

    fa = inv_freq(2 * MOBA_ROPE_HALF)
    fb = inv_freq(2 * MLA_ROPE_HALF)
    pad = jnp.zeros((LANES - 2 * MOBA_ROPE_HALF - 2 * MLA_ROPE_HALF,), F32)
    return jnp.concatenate([fa, fa, fb, fb, pad]).reshape(1, LANES)


def kernel(x, positions, attn_norm, w_in, q_norm, w_uq, kv_norm, w_ukv, w_branch_a, w_branch_b,
           w_out, ffn_norm, w_gate, w_up, w_down, final_norm):
    assert x.shape == (1, SEQ, D_MODEL) and w_in.shape[0] == 1
    x2 = x.reshape(SEQ, D_MODEL)
    pos = positions.reshape(SEQ, 1)
    row = lambda v: v.reshape(1, -1).astype(F32)

    w = w_in[0]
    c0 = 3 * WIDTH
    c1 = c0 + MLA_Q_RANK + MLA_KV_RANK + MLA_ROPE_DIM
    w_q = w[:, :WIDTH].astype(BF16)
    w_k = w[:, WIDTH:2 * WIDTH].astype(BF16)
    w_vt =w[:, 2 * WIDTH:c0].T.astype(BF16)
    w_c = jnp.pad(w[:, c0:c1], ((0, 0), (0, LANES - MLA_ROPE_DIM))).astype(BF16)
    w_g = w[:, c1:].astype(BF16)
    uq = w_uq[0].reshape(MLA_Q_RANK, HEADS, HEAD_DIM + MLA_ROPE_DIM)
    uq_rope = jnp.pad(uq[:, :, HEAD_DIM:], ((0, 0), (0, 0), (0, LANES - MLA_ROPE_DIM)))
    w_uq2 = jnp.concatenate([uq[:, :, :HEAD_DIM].reshape(MLA_Q_RANK, WIDTH),
                             uq_rope.reshape(MLA_Q_RANK, WIDTH)], axis=1).astype(BF16)
    ukv = w_ukv[0].reshape(MLA_KV_RANK, HEADS, 2 * HEAD_DIM)
    w_uk = ukv[:, :, :HEAD_DIM].reshape(MLA_KV_RANK, WIDTH).astype(BF16)
    w_uvt = ukv[:, :, HEAD_DIM:].reshape(MLA_KV_RANK, WIDTH).T.astype(BF16)

    g_attn = row(attn_norm[0])
    q_a, k_a, vt_a, bias, qn_b, qr_b, kn_b, vt_b, krx = _in_proj(
        x2, g_attn, pos, _rope_freqs(), w_q, w_k, w_vt,
        w_c, row(q_norm[0]), w_uq2, row(kv_norm[0]), w_uk, w_uvt, FLASH_TK)

    blk =jnp.arange(SEQ, dtype=jnp.int32)[:, None] // MOBA_BLOCK
    onehot = (blk == jnp.arange(LANES, dtype=jnp.int32)[None, :]).astype(BF16)

    y_a = _flash(q_a, bias, k_a, onehot, vt_a, "flash_moba", FLASH_TQ, FLASH_TK, FLASH_GROUP)
    y_b = _flash(qn_b, qr_b, kn_b, krx, vt_b, "flash_mla", FLASH_TQ, FLASH_TK, FLASH_GROUP)

    mixed = _gated_merge(x2, g_attn, w_g, y_a, y_b,
                         w_branch_a[0].astype(BF16), w_branch_b[0].astype(BF16))
    h = _out_proj(mixed, w_out[0].astype(BF16), x2)
    out = _ffn(h, row(ffn_norm[0]), w_gate[0].astype(BF16), w_up[0].astype(BF16),
               w_down[0].astype(BF16), row(final_norm))
    return out.reshape(1, SEQ, D_MODEL)
```

```python
import functools

import jax
import jax.numpy as jnp
from jax import lax
from jax.experimental import pallas as pl
from jax.experimental.pallas import tpu as pltpu

F32 = jnp.float32
BF16 = jnp.bfloat16

D_MODEL = 2048
SEQ = 16384
HEADS = 8
HEAD_DIM = 128
WIDTH = HEADS * HEAD_DIM
MOBA_BLOCK = 256
MOBA_TOPK = 3
N_BLOCKS = SEQ // MOBA_BLOCK
MOBA_ROPE_HALF = 16
MLA_ROPE_DIM = 64
MLA_ROPE_HALF = 32
MLA_Q_RANK = 512
MLA_KV_RANK = 256
ROPE_THETA = 500000.0
D_FF = 5632
EPS = 1e-6
NEG = -1e30
LANES = 128
SUBLANES = 8
FLASH_TQ = 1024
FLASH_TK = 512
FLASH_CHUNK = 64
FLASH_GROUP = 2
FLASH_QTILES = 2
MERGE_CHUNK = 512

LOG2E = 1.4426950408889634
MOBA_SCALE = HEAD_DIM ** -0.5 * LOG2E
MLA_SCALE = (HEAD_DIM + MLA_ROPE_DIM) ** -0.5 * LOG2E

VMEM_LIMIT = 52 * 1024 * 1024
FLASH_VMEM_LIMIT = 60 * 1024 * 1024


def _params(*sem, flags=None):
    return pltpu.CompilerParams(dimension_semantics=sem, vmem_limit_bytes=VMEM_LIMIT, flags=flags)


def _rmsnorm_bf16(x, g):
    ms = jnp.mean(x * x, axis=-1, keepdims=True)
    return ((x * lax.rsqrt(ms + EPS)) * g).astype(BF16)


def _rope(x, cos, sin, half):
    lane = lax.broadcasted_iota(jnp.int32, x.shape, 1)
    partner = jnp.where(lane < half, pltpu.roll(x, LANES - half, 1), pltpu.roll(x, half, 1))
    return x * cos + partner * sin


def _rope_tables(pos, freqs):
    ang = pos.astype(F32) * freqs
    c = jnp.cos(ang)
    s = jnp.sin(ang)
    lane = lax.broadcasted_iota(jnp.int32, c.shape, 1)
    ha, hb = MOBA_ROPE_HALF, MLA_ROPE_HALF
    cos_a = jnp.where(lane < 2 * ha, c, 1.0)
    sin_a = jnp.where(lane < ha, -s, jnp.where(lane < 2 * ha, s, 0.0))
    c2 = pltpu.roll(c, LANES - 2 * ha, 1)
    s2 = pltpu.roll(s, LANES - 2 * ha, 1)
    cos_b = jnp.where(lane < 2 * hb, c2, 1.0)
    sin_b = jnp.where(lane < hb, -s2, jnp.where(lane < 2 * hb, s2, 0.0))
    return cos_a, sin_a, cos_b, sin_b


def _moba_bias_t(gate, first_query):
    blk = lax.broadcasted_iota(jnp.int32, gate.shape, 0)
    qry = lax.broadcasted_iota(jnp.int32, gate.shape, 1)
    own = (first_query + qry) // MOBA_BLOCK
    past = blk < own
    g = jnp.where(past, gate, -jnp.inf)
    sel = blk == own
    for _ in range(MOBA_TOPK):
        mx = jnp.max(g, axis=0, keepdims=True)
        cand = jnp.where((g == mx) & past, blk, N_BLOCKS)
        idx = jnp.min(cand, axis=0, keepdims=True)
        hit = blk == idx
        sel = sel | hit
        g = jnp.where(hit, -jnp.inf, g)
    bias = jnp.where(sel, 0.0, NEG)
    return jnp.concatenate([bias, jnp.zeros((LANES - N_BLOCKS, gate.shape[1]), F32)], axis=0)


_NT = (((1,), (1,)), ((), ()))


def _head_major(tm):
    return pl.BlockSpec((HEADS, tm, HEAD_DIM), lambda i, *_: (0, i, 0))


def _store_vt(vt_ref, vt):
    for h in range(HEADS):
        vt_ref[h, 0] = vt[h * HEAD_DIM:(h + 1) * HEAD_DIM, :].astype(BF16)


def _resident(shape):
    return pl.BlockSpec(shape, lambda i: (0,) * len(shape), pipeline_mode=pl.Buffered(1))


def _inproj_kernel(x_ref, g_ref, pos_ref, freq_ref, wq_ref, wk_ref, wvt_ref,
                   wc_ref, qg_ref, wuq_ref, kvg_ref, wuk_ref, wuvt_ref,
                   q_ref, k_ref, vta_ref, bias_ref,
                   qn_ref, qr_ref, kn_ref, vt_ref, kr_ref, kmean_ref):
    i = pl.program_id(0)
    tm = x_ref.shape[0]
    xn = _rmsnorm_bf16(x_ref[...], g_ref[...])
    cos, sin, cos_b, sin_b = _rope_tables(pos_ref[...], freq_ref[...])

    @pl.when(i == 0)
    def _():
        kmean_ref[...] = jnp.zeros_like(kmean_ref)

    k = jnp.dot(xn, wk_ref[...], preferred_element_type=F32)
    slot = lax.broadcasted_iota(jnp.int32, (N_BLOCKS, HEAD_DIM), 0)
    for h in range(HEADS):
        sl = slice(h * HEAD_DIM, (h + 1) * HEAD_DIM)
        r = _rope(k[:, sl], cos, sin, MOBA_ROPE_HALF)
        k_ref[h] = r.astype(BF16)
        means = kmean_ref[:, sl]
        for b in range(tm // MOBA_BLOCK):
            rows = r[b * MOBA_BLOCK:(b + 1) * MOBA_BLOCK]
            means = jnp.where(slot == i * (tm // MOBA_BLOCK) + b,
                              jnp.mean(rows, axis=0, keepdims=True), means)
        kmean_ref[:, sl] = means
    q = jnp.dot(xn, wq_ref[...], preferred_element_type=F32)
    for h in range(HEADS):
        sl = slice(h * HEAD_DIM, (h + 1) * HEAD_DIM)
        qh = _rope(q[:, sl], cos, sin, MOBA_ROPE_HALF) * MOBA_SCALE
        q_ref[h] = qh.astype(BF16)
        gate = lax.dot_general(kmean_ref[:, sl], qh, _NT, preferred_element_type=F32,
                               precision=lax.Precision.HIGHEST)
        bias_ref[h] = jnp.transpose(_moba_bias_t(gate, i * tm)).astype(BF16)
    _store_vt(vta_ref, lax.dot_general(wvt_ref[...], xn, _NT, preferred_element_type=F32))

    c = jnp.dot(xn, wc_ref[...], preferred_element_type=F32)
    cos, sin = cos_b, sin_b
    kr = c[:, MLA_Q_RANK + MLA_KV_RANK:]
    kr_ref[...] = _rope(kr, cos, sin, MLA_ROPE_HALF).astype(BF16)

    cq = _rmsnorm_bf16(c[:, :MLA_Q_RANK], qg_ref[...])
    q = jnp.dot(cq, wuq_ref[...], preferred_element_type=F32)
    ckv = _rmsnorm_bf16(c[:, MLA_Q_RANK:MLA_Q_RANK + MLA_KV_RANK], kvg_ref[...])
    kn = jnp.dot(ckv, wuk_ref[...], preferred_element_type=F32)
    for h in range(HEADS):
        sl = slice(h * HEAD_DIM, (h + 1) * HEAD_DIM)
        sl_r = slice(WIDTH + h * HEAD_DIM, WIDTH + (h + 1) * HEAD_DIM)
        qn_ref[h] = (q[:, sl] * MLA_SCALE).astype(BF16)
        qr_ref[h] = (_rope(q[:, sl_r], cos, sin, MLA_ROPE_HALF) * MLA_SCALE).astype(BF16)
        kn_ref[h] = kn[:, sl].astype(BF16)
    _store_vt(vt_ref, lax.dot_general(wuvt_ref[...], ckv, _NT, preferred_element_type=F32))


def _in_proj(x, g, pos, freqs, w_q, w_k, w_vt, w_c, qg, w_uq, kvg, w_uk, w_uvt, tm):
    rows = lambda width: pl.BlockSpec((tm, width), lambda i: (i, 0))
    vt_spec = pl.BlockSpec((HEADS, 1, HEAD_DIM, tm), lambda i: (0, i, 0, 0))
    head_major = jax.ShapeDtypeStruct((HEADS, SEQ, HEAD_DIM), BF16)
    vt_shape = jax.ShapeDtypeStruct((HEADS, SEQ // tm, HEAD_DIM, tm), BF16)
    return pl.pallas_call(
        _inproj_kernel,
        grid=(SEQ // tm,),
        in_specs=[rows(D_MODEL), _resident((1, D_MODEL)), rows(1), _resident((1, LANES)),
                  _resident((D_MODEL, WIDTH)), _resident((D_MODEL, WIDTH)),
                  _resident((WIDTH, D_MODEL)),
                  _resident((D_MODEL, w_c.shape[1])),
                  _resident((1, MLA_Q_RANK)), _resident((MLA_Q_RANK, 2 * WIDTH)),
                  _resident((1, MLA_KV_RANK)), _resident((MLA_KV_RANK, WIDTH)),
                  _resident((WIDTH, MLA_KV_RANK))],
        out_specs=[_head_major(tm), _head_major(tm), vt_spec, _head_major(tm),
                   _head_major(tm), _head_major(tm), _head_major(tm), vt_spec, rows(LANES)],
        out_shape=[head_major, head_major, vt_shape, head_major,
                   head_major, head_major, head_major, vt_shape,
                   jax.ShapeDtypeStruct((SEQ, LANES), BF16)],
        scratch_shapes=[pltpu.VMEM((N_BLOCKS, WIDTH), F32)],
        compiler_params=_params("arbitrary"),
        name="in_proj",
    )(x, g, pos, freqs, w_q, w_k, w_vt, w_c, qg, w_uq, kvg, w_uk, w_uvt)


FLASH_STREAM_SCRATCH = 10


def _flash_kernel(q_ref, qx_ref, k_ref, kx_ref, vt_ref, o_ref, *scratch, tq, tk):
    n_heads, n_qtiles = q_ref.shape[0], q_ref.shape[1] // tq
    assert n_qtiles in (1, 2)
    i0 = pl.program_id(1) * n_qtiles
    streams = [[], []]
    for h in range(n_heads):
        for t in range(n_qtiles):
            rows = pl.ds(t * tq, tq)
            n = len(streams[0]) + len(streams[1])
            streams[t].append(_flash_stream(
                q_ref.at[h, rows], qx_ref.at[h, rows], k_ref.at[h], kx_ref, vt_ref.at[h],
                o_ref.at[h, rows],
                *scratch[n * FLASH_STREAM_SCRATCH:(n + 1) * FLASH_STREAM_SCRATCH],
                q_tile=i0 + t, tq=tq, tk=tk))
    everyone = streams[0] + streams[1]

    def on(group, stage, *args, **kwargs):
        for stream in group:
            getattr(stream, stage)(*args, **kwargs)

    on(everyone, "score_stage", 0, 0, masked=True)
    on(everyone, "score_stage", 1, 1, masked=True)
    on(everyone, "init")
    on(everyone, "exp_stage", 0)

    def step(group, k, a, last=False):
        if not last:
            on(group, "score_stage", k + 2, a)
        on(group, "exp_stage", 1 - a)
        on(group, "value_stage", k, a)

    def pair(kk, carry):
        step(everyone, 2 * kk, 0)
        step(everyone, 2 * kk + 1, 1)
        return carry

    lax.fori_loop(0, i0, pair, 0)
    step(streams[0], 2 * i0, 0, last=True)
    if streams[1]:
        step(streams[1], 2 * i0, 0)
    on(streams[0], "value_stage", 2 * i0 + 1, 1)
    on(streams[0], "finish")
    if streams[1]:
        step(streams[1], 2 * i0 + 1, 1)
        step(streams[1], 2 * i0 + 2, 0, last=True)
        on(streams[1], "value_stage", 2 * i0 + 3, 1)
        on(streams[1], "finish")


class _FlashStream:
    pass


def _flash_stream(q_ref, qx_ref, k_ref, kx_ref, vt_ref, o_ref,
                  s0_ref, s1_ref, p0_ref, p1_ref, acc_ref,
                  m_ref, max0_ref, max1_ref, alpha0_ref, alpha1_ref, *, q_tile, tq, tk):
    i = q_tile
    diag = tq // tk
    assert diag == 2
    qa = jnp.concatenate([q_ref[...], qx_ref[...]], axis=1)

    def tile_of(n):
        return jnp.where(n < diag, i * diag + n, n - diag)

    def scores(n):
        rows = pl.ds(pl.multiple_of(tile_of(n) * tk, tk), tk)
        ka = jnp.concatenate([k_ref[rows, :], kx_ref[rows, :]], axis=1)
        return lax.dot_general(ka, qa, _NT, preferred_element_type=F32)

    def diag_scores(n):
        s = scores(n)
        key = lax.broadcasted_iota(jnp.int32, s.shape, 0) + n * tk
        qry = lax.broadcasted_iota(jnp.int32, s.shape, 1)
        return jnp.where(key <= qry, s, NEG)

    chunks = [slice(c * FLASH_CHUNK, (c + 1) * FLASH_CHUNK) for c in range(tk // FLASH_CHUNK)]

    def fold8(x, op):
        out = x[0:SUBLANES]
        for r in range(1, x.shape[0] // SUBLANES):
            out = op(out, x[r * SUBLANES:(r + 1) * SUBLANES])
        return out

    def rep(x8, rows):
        return jnp.concatenate([x8] * (rows // SUBLANES), axis=0)

    s_slot, p_slot = (s0_ref, s1_ref), (p0_ref, p1_ref)
    max_slot, alpha_slot = (max0_ref, max1_ref), (alpha0_ref, alpha1_ref)
    ones_rows = jnp.ones((acc_ref.shape[0] - HEAD_DIM, tk), BF16)

    def score_stage(n, slot, masked=False):
        val = diag_scores(n) if masked else scores(n)
        s_slot[slot][...] = val
        mx = fold8(val[chunks[0], :], jnp.maximum)
        for c in chunks[1:]:
            mx = jnp.maximum(mx, fold8(val[c, :], jnp.maximum))
        max_slot[slot][...] = jnp.broadcast_to(jnp.max(mx, axis=0, keepdims=True), mx.shape)

    def exp_stage(slot):
        m = m_ref[...]
        m_new = jnp.maximum(m, max_slot[slot][...])
        m_ref[...] = m_new
        alpha_slot[slot][...] = jnp.exp2(m - m_new)
        m_rep = rep(m_new, FLASH_CHUNK)
        for c in chunks:
            shifted = s_slot[slot][c, :] - m_rep
            p_slot[slot][c, :] = jnp.exp2(shifted.astype(BF16))

    def value_stage(n, slot):
        alpha = rep(alpha_slot[slot][...], acc_ref.shape[0])
        vt1 = jnp.concatenate([vt_ref[tile_of(n)], ones_rows], axis=0)
        pv = jnp.dot(vt1, p_slot[slot][...], preferred_element_type=F32)
        acc_ref[...] = alpha * acc_ref[...] + pv

    def init():
        acc_ref[...] = jnp.zeros_like(acc_ref)
        m_ref[...] = jnp.full((SUBLANES, tq), NEG, F32)

    def finish():
        acc = acc_ref[...]
        out_t = acc[:HEAD_DIM] / acc[HEAD_DIM:HEAD_DIM + 1]
        o_ref[...] = jnp.transpose(out_t).astype(o_ref.dtype)

    stream = _FlashStream()
    stream.score_stage, stream.exp_stage, stream.value_stage = score_stage, exp_stage, value_stage
    stream.init, stream.finish = init, finish
    return stream


def _flash(q_arr, qx_arr, k_arr, kx_arr, vt_arr, name, tq, tk, group, qtiles):
    per_group = lambda rows: pl.BlockSpec((group, rows, HEAD_DIM), lambda g, i: (g, i, 0))
    tq_step = tq * qtiles
    whole = lambda shape: pl.BlockSpec(shape, lambda g, i: (g,) + (0,) * (len(shape) - 1),
                                       pipeline_mode=pl.Buffered(1))
    stream_scratch = (
        [pltpu.VMEM((tk, tq), F32)] * 2
        + [pltpu.VMEM((tk, tq), BF16)] * 2
        + [pltpu.VMEM((HEAD_DIM + 2 * SUBLANES, tq), F32)]
        + [pltpu.VMEM((SUBLANES, tq), F32)] * 5)
    assert len(stream_scratch) == FLASH_STREAM_SCRATCH
    return pl.pallas_call(
        functools.partial(_flash_kernel, tq=tq, tk=tk),
        grid=(HEADS // group, SEQ // tq_step),
        in_specs=[per_group(tq_step), per_group(tq_step),
                  whole((group, SEQ, HEAD_DIM)),
                  pl.BlockSpec((SEQ, LANES), lambda g, i: (0, 0), pipeline_mode=pl.Buffered(1)),
                  whole((group, SEQ // tk, HEAD_DIM, tk))],
        out_specs=per_group(tq_step),
        out_shape=jax.ShapeDtypeStruct((HEADS, SEQ, HEAD_DIM), BF16),
        scratch_shapes=stream_scratch * (group * qtiles),
        compiler_params=pltpu.CompilerParams(dimension_semantics=("parallel", "arbitrary"),
                                             vmem_limit_bytes=FLASH_VMEM_LIMIT),
        name=name,
    )(q_arr, qx_arr, k_arr, kx_arr, vt_arr)


def _merge_kernel(x_ref, g_ref, wg_ref, ya_ref, yb_ref, wa_ref, wb_ref, o_ref):
    xn = _rmsnorm_bf16(x_ref[...], g_ref[...])
    ya = jnp.concatenate([ya_ref[h] for h in range(HEADS)], axis=1)
    yb = jnp.concatenate([yb_ref[h] for h in range(HEADS)], axis=1)
    for c in range(D_MODEL // MERGE_CHUNK):
        cols = slice(c * MERGE_CHUNK, (c + 1) * MERGE_CHUNK)
        cols_b = slice(D_MODEL + c * MERGE_CHUNK, D_MODEL + (c + 1) * MERGE_CHUNK)
        gate_a = jax.nn.sigmoid(jnp.dot(xn, wg_ref[:, cols], preferred_element_type=F32))
        gate_b = jax.nn.sigmoid(jnp.dot(xn, wg_ref[:, cols_b], preferred_element_type=F32))
        a = jnp.dot(ya, wa_ref[:, cols], preferred_element_type=F32)
        b = jnp.dot(yb, wb_ref[:, cols], preferred_element_type=F32)
        o_ref[:, cols] = (gate_a * a + gate_b * b).astype(BF16)


def _gated_merge(x, g, w_g, ya, yb, wa, wb, tm=512):
    return pl.pallas_call(
        _merge_kernel,
        grid=(SEQ // tm,),
        in_specs=[pl.BlockSpec((tm, D_MODEL), lambda i: (i, 0)),
                  _resident((1, D_MODEL)), _resident((D_MODEL, 2 * D_MODEL)),
                  _head_major(tm), _head_major(tm),
                  _resident((WIDTH, D_MODEL)), _resident((WIDTH, D_MODEL))],
        out_specs=pl.BlockSpec((tm, D_MODEL), lambda i: (i, 0)),
        out_shape=jax.ShapeDtypeStruct((SEQ, D_MODEL), BF16),
        compiler_params=_params("parallel"),
        name="gated_merge",
    )(x, g, w_g, ya, yb, wa, wb)


def _out_kernel(m_ref, w_ref, x_ref, o_ref):
    o_ref[...] = x_ref[...] + jnp.dot(m_ref[...], w_ref[...], preferred_element_type=F32)


def _out_proj(mixed, w_out, x, tm=512):
    return pl.pallas_call(
        _out_kernel,
        grid=(SEQ // tm,),
        in_specs=[pl.BlockSpec((tm, D_MODEL), lambda i: (i, 0)),
                  _resident((D_MODEL, D_MODEL)),
                  pl.BlockSpec((tm, D_MODEL), lambda i: (i, 0))],
        out_specs=pl.BlockSpec((tm, D_MODEL), lambda i: (i, 0)),
        out_shape=jax.ShapeDtypeStruct((SEQ, D_MODEL), F32),
        compiler_params=_params("parallel"),
        name="out_proj",
    )(mixed, w_out, x)


def _ffn_kernel(h_ref, g_ref, wg_ref, wu_ref, wd_ref, fg_ref, o_ref, hn_ref, acc_ref):
    f = pl.program_id(1)

    @pl.when(f == 0)
    def _():
        hn_ref[...] = _rmsnorm_bf16(h_ref[...], g_ref[...])
        acc_ref[...] = jnp.zeros_like(acc_ref)

    hn = hn_ref[...]
    gate = jnp.dot(hn, wg_ref[...], preferred_element_type=F32)
    up = jnp.dot(hn, wu_ref[...], preferred_element_type=F32)
    act = (jax.nn.silu(gate) * up).astype(BF16)
    acc_ref[...] += jnp.dot(act, wd_ref[...], preferred_element_type=F32)

    @pl.when(f == pl.num_programs(1) - 1)
    def _():
        y = h_ref[...] + acc_ref[...]
        ms = jnp.mean(y * y, axis=-1, keepdims=True)
        o_ref[...] = (y * lax.rsqrt(ms + EPS)) * fg_ref[...]


def _ffn(h, g, wg, wu, wd, fg, tm=512, tf=512):
    return pl.pallas_call(
        _ffn_kernel,
        grid=(SEQ // tm, D_FF // tf),
        in_specs=[pl.BlockSpec((tm, D_MODEL), lambda i, f: (i, 0)),
                  pl.BlockSpec((1, D_MODEL), lambda i, f: (0, 0)),
                  pl.BlockSpec((D_MODEL, tf), lambda i, f: (0, f)),
                  pl.BlockSpec((D_MODEL, tf), lambda i, f: (0, f)),
                  pl.BlockSpec((tf, D_MODEL), lambda i, f: (f, 0)),
                  pl.BlockSpec((1, D_MODEL), lambda i, f: (0, 0))],
        out_specs=pl.BlockSpec((tm, D_MODEL), lambda i, f: (i, 0)),
        out_shape=jax.ShapeDtypeStruct((SEQ, D_MODEL), F32),
        scratch_shapes=[pltpu.VMEM((tm, D_MODEL), BF16),
                        pltpu.VMEM((tm, D_MODEL), F32)],
        compiler_params=_params("parallel", "arbitrary"),
        name="ffn",
    )(h, g, wg, wu, wd, fg)


def _rope_freqs():
    def inv_freq(d):
        return ROPE_THETA ** (-jnp.arange(0, d, 2, dtype=F32) / d)

    fa = inv_freq(2 * MOBA_ROPE_HALF)
    fb = inv_freq(2 * MLA_ROPE_HALF)
    pad = jnp.zeros((LANES - 2 * MOBA_ROPE_HALF - 2 * MLA_ROPE_HALF,), F32)
    return jnp.concatenate([fa, fa, fb, fb, pad]).reshape(1, LANES)


def kernel(x, positions, attn_norm, w_in, q_norm, w_uq, kv_norm, w_ukv, w_branch_a, w_branch_b,
           w_out, ffn_norm, w_gate, w_up, w_down, final_norm):
    assert x.shape == (1, SEQ, D_MODEL) and w_in.shape[0] == 1
    x2 = x.reshape(SEQ, D_MODEL)
    pos = positions.reshape(SEQ, 1)
    row = lambda v: v.reshape(1, -1).astype(F32)

    w = w_in[0]
    c0 = 3 * WIDTH
    c1 = c0 + MLA_Q_RANK + MLA_KV_RANK + MLA_ROPE_DIM
    w_q = w[:, :WIDTH].astype(BF16)
    w_k = w[:, WIDTH:2 * WIDTH].astype(BF16)
    w_vt =w[:, 2 * WIDTH:c0].T.astype(BF16)
    w_c = jnp.pad(w[:, c0:c1], ((0, 0), (0, LANES - MLA_ROPE_DIM))).astype(BF16)
    w_g = w[:, c1:].astype(BF16)
    uq = w_uq[0].reshape(MLA_Q_RANK, HEADS, HEAD_DIM + MLA_ROPE_DIM)
    uq_rope = jnp.pad(uq[:, :, HEAD_DIM:], ((0, 0), (0, 0), (0, LANES - MLA_ROPE_DIM)))
    w_uq2 = jnp.concatenate([uq[:, :, :HEAD_DIM].reshape(MLA_Q_RANK, WIDTH),
                             uq_rope.reshape(MLA_Q_RANK, WIDTH)], axis=1).astype(BF16)
    ukv = w_ukv[0].reshape(MLA_KV_RANK, HEADS, 2 * HEAD_DIM)
    w_uk = ukv[:, :, :HEAD_DIM].reshape(MLA_KV_RANK, WIDTH).astype(BF16)
    w_uvt = ukv[:, :, HEAD_DIM:].reshape(MLA_KV_RANK, WIDTH).T.astype(BF16)

    g_attn = row(attn_norm[0])
    q_a, k_a, vt_a, bias, qn_b, qr_b, kn_b, vt_b, krx = _in_proj(
        x2, g_attn, pos, _rope_freqs(), w_q, w_k, w_vt,
        w_c, row(q_norm[0]), w_uq2, row(kv_norm[0]), w_uk, w_uvt, FLASH_TK)

    blk =jnp.arange(SEQ, dtype=jnp.int32)[:, None] // MOBA_BLOCK
    onehot = (blk == jnp.arange(LANES, dtype=jnp.int32)[None, :]).astype(BF16)

    tiling = (FLASH_TQ, FLASH_TK, FLASH_GROUP, FLASH_QTILES)
    y_a = _flash(q_a, bias, k_a, onehot, vt_a, "flash_moba", *tiling)
    y_b = _flash(qn_b, qr_b, kn_b, krx, vt_b, "flash_mla", *tiling)

    mixed = _gated_merge(x2, g_attn, w_g, y_a, y_b,
                         w_branch_a[0].astype(BF16), w_branch_b[0].astype(BF16))
    h = _out_proj(mixed, w_out[0].astype(BF16), x2)
    out = _ffn(h, row(ffn_norm[0]), w_gate[0].astype(BF16), w_up[0].astype(BF16),
               w_down[0].astype(BF16), row(final_norm))
    return out.reshape(1, SEQ, D_MODEL)
```

```python
import functools

import jax
import jax.numpy as jnp
from jax import lax
from jax.experimental import pallas as pl
from jax.experimental.pallas import tpu as pltpu

F32 = jnp.float32
BF16 = jnp.bfloat16

D_MODEL = 2048
SEQ = 16384
HEADS = 8
HEAD_DIM = 128
WIDTH = HEADS * HEAD_DIM
MOBA_BLOCK = 256
MOBA_TOPK = 3
N_BLOCKS = SEQ // MOBA_BLOCK
MOBA_ROPE_HALF = 16
MLA_ROPE_DIM = 64
MLA_ROPE_HALF = 32
MLA_Q_RANK = 512
MLA_KV_RANK = 256
ROPE_THETA = 500000.0
D_FF = 5632
EPS = 1e-6
NEG = -1e30
LANES = 128
SUBLANES = 8
FLASH_TQ = 1024
FLASH_TK = 512
FLASH_CHUNK = 64
FLASH_GROUP = 2
FLASH_QTILES = 2
MERGE_CHUNK = 512

LOG2E = 1.4426950408889634
MOBA_SCALE = HEAD_DIM ** -0.5 * LOG2E
MLA_SCALE = (HEAD_DIM + MLA_ROPE_DIM) ** -0.5 * LOG2E

VMEM_LIMIT = 52 * 1024 * 1024
FLASH_VMEM_LIMIT = 60 * 1024 * 1024


def _params(*sem, flags=None):
    return pltpu.CompilerParams(dimension_semantics=sem, vmem_limit_bytes=VMEM_LIMIT, flags=flags)


def _rmsnorm_bf16(x, g):
    ms = jnp.mean(x * x, axis=-1, keepdims=True)
    return ((x * lax.rsqrt(ms + EPS)) * g).astype(BF16)


def _rope(x, cos, sin, half):
    lane = lax.broadcasted_iota(jnp.int32, x.shape, 1)
    partner = jnp.where(lane < half, pltpu.roll(x, LANES - half, 1), pltpu.roll(x, half, 1))
    return x * cos + partner * sin


def _rope_tables(pos, freqs):
    ang = pos.astype(F32) * freqs
    c = jnp.cos(ang)
    s = jnp.sin(ang)
    lane = lax.broadcasted_iota(jnp.int32, c.shape, 1)
    ha, hb = MOBA_ROPE_HALF, MLA_ROPE_HALF
    cos_a = jnp.where(lane < 2 * ha, c, 1.0)
    sin_a = jnp.where(lane < ha, -s, jnp.where(lane < 2 * ha, s, 0.0))
    c2 = pltpu.roll(c, LANES - 2 * ha, 1)
    s2 = pltpu.roll(s, LANES - 2 * ha, 1)
    cos_b = jnp.where(lane < 2 * hb, c2, 1.0)
    sin_b = jnp.where(lane < hb, -s2, jnp.where(lane < 2 * hb, s2, 0.0))
    return cos_a, sin_a, cos_b, sin_b


def _moba_bias_t(gate, first_query):
    blk = lax.broadcasted_iota(jnp.int32, gate.shape, 0)
    qry = lax.broadcasted_iota(jnp.int32, gate.shape, 1)
    own = (first_query + qry) // MOBA_BLOCK
    past = blk < own
    g = jnp.where(past, gate, -jnp.inf)
    sel = blk == own
    for _ in range(MOBA_TOPK):
        mx = jnp.max(g, axis=0, keepdims=True)
        cand = jnp.where((g == mx) & past, blk, N_BLOCKS)
        idx = jnp.min(cand, axis=0, keepdims=True)
        hit = blk == idx
        sel = sel | hit
        g = jnp.where(hit, -jnp.inf, g)
    bias = jnp.where(sel, 0.0, NEG)
    return jnp.concatenate([bias, jnp.zeros((LANES - N_BLOCKS, gate.shape[1]), F32)], axis=0)


_NT = (((1,), (1,)), ((), ()))


def _head_major(tm):
    return pl.BlockSpec((HEADS, tm, HEAD_DIM), lambda i, *_: (0, i, 0))


def _store_vt(vt_ref, vt):
    for h in range(HEADS):
        vt_ref[h, 0] = vt[h * HEAD_DIM:(h + 1) * HEAD_DIM, :].astype(BF16)


def _resident(shape):
    return pl.BlockSpec(shape, lambda i: (0,) * len(shape), pipeline_mode=pl.Buffered(1))


def _inproj_kernel(x_ref, g_ref, pos_ref, freq_ref, wq_ref, wk_ref, wvt_ref,
                   wc_ref, qg_ref, wuq_ref, kvg_ref, wuk_ref, wuvt_ref,
                   q_ref, k_ref, vta_ref, bias_ref,
                   qn_ref, qr_ref, kn_ref, vt_ref, kr_ref, kmean_ref):
    i = pl.program_id(0)
    tm = x_ref.shape[0]

    @pl.when(i == 0)
    def _():
        kmean_ref[...] = jnp.zeros_like(kmean_ref)

    xn = _rmsnorm_bf16(x_ref[...], g_ref[...])

    k = jnp.dot(xn, wk_ref[...], preferred_element_type=F32)
    cos, sin, cos_b, sin_b = _rope_tables(pos_ref[...], freq_ref[...])
    slot =lax.broadcasted_iota(jnp.int32, (N_BLOCKS, HEAD_DIM), 0)
    for h in range(HEADS):
        sl = slice(h * HEAD_DIM, (h + 1) * HEAD_DIM)
        r = _rope(k[:, sl], cos, sin, MOBA_ROPE_HALF)
        k_ref[h] = r.astype(BF16)
        means = kmean_ref[:, sl]
        for b in range(tm // MOBA_BLOCK):
            rows = r[b * MOBA_BLOCK:(b + 1) * MOBA_BLOCK]
            means = jnp.where(slot == i * (tm // MOBA_BLOCK) + b,
                              jnp.mean(rows, axis=0, keepdims=True), means)
        kmean_ref[:, sl] = means
    q = jnp.dot(xn, wq_ref[...], preferred_element_type=F32)
    for h in range(HEADS):
        sl = slice(h * HEAD_DIM, (h + 1) * HEAD_DIM)
        qh = _rope(q[:, sl], cos, sin, MOBA_ROPE_HALF) * MOBA_SCALE
        q_ref[h] = qh.astype(BF16)
        gate = lax.dot_general(kmean_ref[:, sl], qh, _NT, preferred_element_type=F32,
                               precision=lax.Precision.HIGHEST)
        bias_ref[h] = jnp.transpose(_moba_bias_t(gate, i * tm)).astype(BF16)
    _store_vt(vta_ref, lax.dot_general(wvt_ref[...], xn, _NT, preferred_element_type=F32))

    c = jnp.dot(xn, wc_ref[...], preferred_element_type=F32)
    cos, sin = cos_b, sin_b
    kr = c[:, MLA_Q_RANK + MLA_KV_RANK:]
    kr_ref[...] = _rope(kr, cos, sin, MLA_ROPE_HALF).astype(BF16)

    cq = _rmsnorm_bf16(c[:, :MLA_Q_RANK], qg_ref[...])
    q = jnp.dot(cq, wuq_ref[...], preferred_element_type=F32)
    ckv = _rmsnorm_bf16(c[:, MLA_Q_RANK:MLA_Q_RANK + MLA_KV_RANK], kvg_ref[...])
    kn = jnp.dot(ckv, wuk_ref[...], preferred_element_type=F32)
    for h in range(HEADS):
        sl = slice(h * HEAD_DIM, (h + 1) * HEAD_DIM)
        sl_r = slice(WIDTH + h * HEAD_DIM, WIDTH + (h + 1) * HEAD_DIM)
        qn_ref[h] = (q[:, sl] * MLA_SCALE).astype(BF16)
        qr_ref[h] = (_rope(q[:, sl_r], cos, sin, MLA_ROPE_HALF) * MLA_SCALE).astype(BF16)
        kn_ref[h] = kn[:, sl].astype(BF16)
    _store_vt(vt_ref, lax.dot_general(wuvt_ref[...], ckv, _NT, preferred_element_type=F32))


def _in_proj(x, g, pos, freqs, w_q, w_k, w_vt, w_c, qg, w_uq, kvg, w_uk, w_uvt, tm):
    rows = lambda width: pl.BlockSpec((tm, width), lambda i: (i, 0))
    vt_spec = pl.BlockSpec((HEADS, 1, HEAD_DIM, tm), lambda i: (0, i, 0, 0))
    head_major = jax.ShapeDtypeStruct((HEADS, SEQ, HEAD_DIM), BF16)
    vt_shape = jax.ShapeDtypeStruct((HEADS, SEQ // tm, HEAD_DIM, tm), BF16)
    return pl.pallas_call(
        _inproj_kernel,
        grid=(SEQ // tm,),
        in_specs=[rows(D_MODEL), _resident((1, D_MODEL)), rows(1), _resident((1, LANES)),
                  _resident((D_MODEL, WIDTH)), _resident((D_MODEL, WIDTH)),
                  _resident((WIDTH, D_MODEL)),
                  _resident((D_MODEL, w_c.shape[1])),
                  _resident((1, MLA_Q_RANK)), _resident((MLA_Q_RANK, 2 * WIDTH)),
                  _resident((1, MLA_KV_RANK)), _resident((MLA_KV_RANK, WIDTH)),
                  _resident((WIDTH, MLA_KV_RANK))],
        out_specs=[_head_major(tm), _head_major(tm), vt_spec, _head_major(tm),
                   _head_major(tm), _head_major(tm), _head_major(tm), vt_spec, rows(LANES)],
        out_shape=[head_major, head_major, vt_shape, head_major,
                   head_major, head_major, head_major, vt_shape,
                   jax.ShapeDtypeStruct((SEQ, LANES), BF16)],
        scratch_shapes=[pltpu.VMEM((N_BLOCKS, WIDTH), F32)],
        compiler_params=_params("arbitrary"),
        name="in_proj",
    )(x, g, pos, freqs, w_q, w_k, w_vt, w_c, qg, w_uq, kvg, w_uk, w_uvt)


FLASH_STREAM_SCRATCH = 10


def _flash_kernel(q_ref, qx_ref, k_ref, kx_ref, vt_ref, o_ref, *scratch, tq, tk):
    n_heads, n_qtiles = q_ref.shape[0], q_ref.shape[1] // tq
    assert n_qtiles in (1, 2)
    i0 = pl.program_id(1) * n_qtiles
    streams = [[], []]
    for h in range(n_heads):
        for t in range(n_qtiles):
            rows = pl.ds(t * tq, tq)
            n = len(streams[0]) + len(streams[1])
            streams[t].append(_flash_stream(
                q_ref.at[h, rows], qx_ref.at[h, rows], k_ref.at[h], kx_ref, vt_ref.at[h],
                o_ref.at[h, rows],
                *scratch[n * FLASH_STREAM_SCRATCH:(n + 1) * FLASH_STREAM_SCRATCH],
                q_tile=i0 + t, tq=tq, tk=tk))
    everyone = streams[0] + streams[1]

    def on(group, stage, *args, **kwargs):
        for stream in group:
            getattr(stream, stage)(*args, **kwargs)

    on(everyone, "score_stage", 0, 0, masked=True)
    on(everyone, "score_stage", 1, 1, masked=True)
    on(everyone, "init")
    on(everyone, "exp_stage", 0)

    def step(group, k, a, last=False):
        if not last:
            on(group, "score_stage", k + 2, a)
        on(group, "exp_stage", 1 - a)
        on(group, "value_stage", k, a)

    def pair(kk, carry):
        step(everyone, 2 * kk, 0)
        step(everyone, 2 * kk + 1, 1)
        return carry

    lax.fori_loop(0, i0, pair, 0)
    step(streams[0], 2 * i0, 0, last=True)
    if streams[1]:
        step(streams[1], 2 * i0, 0)
    on(streams[0], "value_stage", 2 * i0 + 1, 1)
    on(streams[0], "finish")
    if streams[1]:
        step(streams[1], 2 * i0 + 1, 1)
        step(streams[1], 2 * i0 + 2, 0, last=True)
        on(streams[1], "value_stage", 2 * i0 + 3, 1)
        on(streams[1], "finish")


class _FlashStream:
    pass


def _flash_stream(q_ref, qx_ref, k_ref, kx_ref, vt_ref, o_ref,
                  s0_ref, s1_ref, p0_ref, p1_ref, acc_ref,
                  m_ref, max0_ref, max1_ref, alpha0_ref, alpha1_ref, *, q_tile, tq, tk):
    i = q_tile
    diag = tq // tk
    assert diag == 2
    qa = jnp.concatenate([q_ref[...], qx_ref[...]], axis=1)

    def tile_of(n):
        return jnp.where(n < diag, i * diag + n, n - diag)

    def scores(n):
        rows = pl.ds(pl.multiple_of(tile_of(n) * tk, tk), tk)
        ka = jnp.concatenate([k_ref[rows, :], kx_ref[rows, :]], axis=1)
        return lax.dot_general(ka, qa, _NT, preferred_element_type=F32)

    def diag_scores(n):
        s = scores(n)
        key = lax.broadcasted_iota(jnp.int32, s.shape, 0) + n * tk
        qry = lax.broadcasted_iota(jnp.int32, s.shape, 1)
        return jnp.where(key <= qry, s, NEG)

    chunks = [slice(c * FLASH_CHUNK, (c + 1) * FLASH_CHUNK) for c in range(tk // FLASH_CHUNK)]

    def fold8(x, op):
        out = x[0:SUBLANES]
        for r in range(1, x.shape[0] // SUBLANES):
            out = op(out, x[r * SUBLANES:(r + 1) * SUBLANES])
        return out

    def rep(x8, rows):
        return jnp.concatenate([x8] * (rows // SUBLANES), axis=0)

    s_slot, p_slot = (s0_ref, s1_ref), (p0_ref, p1_ref)
    max_slot, alpha_slot = (max0_ref, max1_ref), (alpha0_ref, alpha1_ref)
    ones_rows = jnp.ones((acc_ref.shape[0] - HEAD_DIM, tk), BF16)

    def score_stage(n, slot, masked=False):
        val = diag_scores(n) if masked else scores(n)
        s_slot[slot][...] = val
        mx = fold8(val[chunks[0], :], jnp.maximum)
        for c in chunks[1:]:
            mx = jnp.maximum(mx, fold8(val[c, :], jnp.maximum))
        max_slot[slot][...] = jnp.broadcast_to(jnp.max(mx, axis=0, keepdims=True), mx.shape)

    def exp_stage(slot):
        m = m_ref[...]
        m_new = jnp.maximum(m, max_slot[slot][...])
        m_ref[...] = m_new
        alpha_slot[slot][...] = jnp.exp2(m - m_new)
        m_rep = rep(m_new, FLASH_CHUNK)
        for c in chunks:
            shifted = s_slot[slot][c, :] - m_rep
            p_slot[slot][c, :] = jnp.exp2(shifted.astype(BF16))

    def value_stage(n, slot):
        alpha = rep(alpha_slot[slot][...], acc_ref.shape[0])
        vt1 = jnp.concatenate([vt_ref[tile_of(n)], ones_rows], axis=0)
        pv = jnp.dot(vt1, p_slot[slot][...], preferred_element_type=F32)
        acc_ref[...] = alpha * acc_ref[...] + pv

    def init():
        acc_ref[...] = jnp.zeros_like(acc_ref)
        m_ref[...] = jnp.full((SUBLANES, tq), NEG, F32)

    def finish():
        acc = acc_ref[...]
        out_t = acc[:HEAD_DIM] / acc[HEAD_DIM:HEAD_DIM + 1]
        o_ref[...] = jnp.transpose(out_t).astype(o_ref.dtype)

    stream = _FlashStream()
    stream.score_stage, stream.exp_stage, stream.value_stage = score_stage, exp_stage, value_stage
    stream.init, stream.finish = init, finish
    return stream


def _flash(q_arr, qx_arr, k_arr, kx_arr, vt_arr, name, tq, tk, group, qtiles):
    per_group = lambda rows: pl.BlockSpec((group, rows, HEAD_DIM), lambda g, i: (g, i, 0))
    tq_step = tq * qtiles
    whole = lambda shape: pl.BlockSpec(shape, lambda g, i: (g,) + (0,) * (len(shape) - 1),
                                       pipeline_mode=pl.Buffered(1))
    stream_scratch = (
        [pltpu.VMEM((tk, tq), F32)] * 2
        + [pltpu.VMEM((tk, tq), BF16)] * 2
        + [pltpu.VMEM((HEAD_DIM + 2 * SUBLANES, tq), F32)]
        + [pltpu.VMEM((SUBLANES, tq), F32)] * 5)
    assert len(stream_scratch) == FLASH_STREAM_SCRATCH
    return pl.pallas_call(
        functools.partial(_flash_kernel, tq=tq, tk=tk),
        grid=(HEADS // group, SEQ // tq_step),
        in_specs=[per_group(tq_step), per_group(tq_step),
                  whole((group, SEQ, HEAD_DIM)),
                  pl.BlockSpec((SEQ, LANES), lambda g, i: (0, 0), pipeline_mode=pl.Buffered(1)),
                  whole((group, SEQ // tk, HEAD_DIM, tk))],
        out_specs=per_group(tq_step),
        out_shape=jax.ShapeDtypeStruct((HEADS, SEQ, HEAD_DIM), BF16),
        scratch_shapes=stream_scratch * (group * qtiles),
        compiler_params=pltpu.CompilerParams(dimension_semantics=("parallel", "arbitrary"),
                                             vmem_limit_bytes=FLASH_VMEM_LIMIT),
        name=name,
    )(q_arr, qx_arr, k_arr, kx_arr, vt_arr)


def _merge_kernel(x_ref, g_ref, wg_ref, ya_ref, yb_ref, wa_ref, wb_ref, o_ref):
    xn = _rmsnorm_bf16(x_ref[...], g_ref[...])
    ya = jnp.concatenate([ya_ref[h] for h in range(HEADS)], axis=1)
    yb = jnp.concatenate([yb_ref[h] for h in range(HEADS)], axis=1)
    for c in range(D_MODEL // MERGE_CHUNK):
        cols = slice(c * MERGE_CHUNK, (c + 1) * MERGE_CHUNK)
        cols_b = slice(D_MODEL + c * MERGE_CHUNK, D_MODEL + (c + 1) * MERGE_CHUNK)
        gate_a = jax.nn.sigmoid(jnp.dot(xn, wg_ref[:, cols], preferred_element_type=F32))
        gate_b = jax.nn.sigmoid(jnp.dot(xn, wg_ref[:, cols_b], preferred_element_type=F32))
        a = jnp.dot(ya, wa_ref[:, cols], preferred_element_type=F32)
        b = jnp.dot(yb, wb_ref[:, cols], preferred_element_type=F32)
        o_ref[:, cols] = (gate_a * a + gate_b * b).astype(BF16)


def _gated_merge(x, g, w_g, ya, yb, wa, wb, tm=512):
    return pl.pallas_call(
        _merge_kernel,
        grid=(SEQ // tm,),
        in_specs=[pl.BlockSpec((tm, D_MODEL), lambda i: (i, 0)),
                  _resident((1, D_MODEL)), _resident((D_MODEL, 2 * D_MODEL)),
                  _head_major(tm), _head_major(tm),
                  _resident((WIDTH, D_MODEL)), _resident((WIDTH, D_MODEL))],
        out_specs=pl.BlockSpec((tm, D_MODEL), lambda i: (i, 0)),
        out_shape=jax.ShapeDtypeStruct((SEQ, D_MODEL), BF16),
        compiler_params=_params("parallel"),
        name="gated_merge",
    )(x, g, w_g, ya, yb, wa, wb)


def _out_kernel(m_ref, w_ref, x_ref, o_ref):
    o_ref[...] = x_ref[...] + jnp.dot(m_ref[...], w_ref[...], preferred_element_type=F32)


def _out_proj(mixed, w_out, x, tm=512):
    return pl.pallas_call(
        _out_kernel,
        grid=(SEQ // tm,),
        in_specs=[pl.BlockSpec((tm, D_MODEL), lambda i: (i, 0)),
                  _resident((D_MODEL, D_MODEL)),
                  pl.BlockSpec((tm, D_MODEL), lambda i: (i, 0))],
        out_specs=pl.BlockSpec((tm, D_MODEL), lambda i: (i, 0)),
        out_shape=jax.ShapeDtypeStruct((SEQ, D_MODEL), F32),
        compiler_params=_params("parallel"),
        name="out_proj",
    )(mixed, w_out, x)


def _ffn_kernel(h_ref, g_ref, wg_ref, wu_ref, wd_ref, fg_ref, o_ref, hn_ref, acc_ref):
    f = pl.program_id(1)

    @pl.when(f == 0)
    def _():
        hn_ref[...] = _rmsnorm_bf16(h_ref[...], g_ref[...])
        acc_ref[...] = jnp.zeros_like(acc_ref)

    hn = hn_ref[...]
    gate = jnp.dot(hn, wg_ref[...], preferred_element_type=F32)
    up = jnp.dot(hn, wu_ref[...], preferred_element_type=F32)
    act = (jax.nn.silu(gate) * up).astype(BF16)
    acc_ref[...] += jnp.dot(act, wd_ref[...], preferred_element_type=F32)

    @pl.when(f == pl.num_programs(1) - 1)
    def _():
        y = h_ref[...] + acc_ref[...]
        ms = jnp.mean(y * y, axis=-1, keepdims=True)
        o_ref[...] = (y * lax.rsqrt(ms + EPS)) * fg_ref[...]


def _ffn(h, g, wg, wu, wd, fg, tm=512, tf=512):
    return pl.pallas_call(
        _ffn_kernel,
        grid=(SEQ // tm, D_FF // tf),
        in_specs=[pl.BlockSpec((tm, D_MODEL), lambda i, f: (i, 0)),
                  pl.BlockSpec((1, D_MODEL), lambda i, f: (0, 0)),
                  pl.BlockSpec((D_MODEL, tf), lambda i, f: (0, f)),
                  pl.BlockSpec((D_MODEL, tf), lambda i, f: (0, f)),
                  pl.BlockSpec((tf, D_MODEL), lambda i, f: (f, 0)),
                  pl.BlockSpec((1, D_MODEL), lambda i, f: (0, 0))],
        out_specs=pl.BlockSpec((tm, D_MODEL), lambda i, f: (i, 0)),
        out_shape=jax.ShapeDtypeStruct((SEQ, D_MODEL), F32),
        scratch_shapes=[pltpu.VMEM((tm, D_MODEL), BF16),
                        pltpu.VMEM((tm, D_MODEL), F32)],
        compiler_params=_params("parallel", "arbitrary"),
        name="ffn",
    )(h, g, wg, wu, wd, fg)


def _rope_freqs():
    def inv_freq(d):
        return ROPE_THETA ** (-jnp.arange(0, d, 2, dtype=F32) / d)

    fa = inv_freq(2 * MOBA_ROPE_HALF)
    fb = inv_freq(2 * MLA_ROPE_HALF)
    pad = jnp.zeros((LANES - 2 * MOBA_ROPE_HALF - 2 * MLA_ROPE_HALF,), F32)
    return jnp.concatenate([fa, fa, fb, fb, pad]).reshape(1, LANES)


def kernel(x, positions, attn_norm, w_in, q_norm, w_uq, kv_norm, w_ukv, w_branch_a, w_branch_b,
           w_out, ffn_norm, w_gate, w_up, w_down, final_norm):
    assert x.shape == (1, SEQ, D_MODEL) and w_in.shape[0] == 1
    x2 = x.reshape(SEQ, D_MODEL)
    pos = positions.reshape(SEQ, 1)
    row = lambda v: v.reshape(1, -1).astype(F32)

    w = w_in[0]
    c0 = 3 * WIDTH
    c1 = c0 + MLA_Q_RANK + MLA_KV_RANK + MLA_ROPE_DIM
    w_q = w[:, :WIDTH].astype(BF16)
    w_k = w[:, WIDTH:2 * WIDTH].astype(BF16)
    w_vt =w[:, 2 * WIDTH:c0].T.astype(BF16)
    w_c = jnp.pad(w[:, c0:c1], ((0, 0), (0, LANES - MLA_ROPE_DIM))).astype(BF16)
    w_g = w[:, c1:].astype(BF16)
    uq = w_uq[0].reshape(MLA_Q_RANK, HEADS, HEAD_DIM + MLA_ROPE_DIM)
    uq_rope = jnp.pad(uq[:, :, HEAD_DIM:], ((0, 0), (0, 0), (0, LANES - MLA_ROPE_DIM)))
    w_uq2 = jnp.concatenate([uq[:, :, :HEAD_DIM].reshape(MLA_Q_RANK, WIDTH),
                             uq_rope.reshape(MLA_Q_RANK, WIDTH)], axis=1).astype(BF16)
    ukv = w_ukv[0].reshape(MLA_KV_RANK, HEADS, 2 * HEAD_DIM)
    w_uk = ukv[:, :, :HEAD_DIM].reshape(MLA_KV_RANK, WIDTH).astype(BF16)
    w_uvt = ukv[:, :, HEAD_DIM:].reshape(MLA_KV_RANK, WIDTH).T.astype(BF16)

    g_attn = row(attn_norm[0])
    q_a, k_a, vt_a, bias, qn_b, qr_b, kn_b, vt_b, krx = _in_proj(
        x2, g_attn, pos, _rope_freqs(), w_q, w_k, w_vt,
        w_c, row(q_norm[0]), w_uq2, row(kv_norm[0]), w_uk, w_uvt, FLASH_TK)

    blk =jnp.arange(SEQ, dtype=jnp.int32)[:, None] // MOBA_BLOCK
    onehot = (blk == jnp.arange(LANES, dtype=jnp.int32)[None, :]).astype(BF16)

    tiling = (FLASH_TQ, FLASH_TK, FLASH_GROUP, FLASH_QTILES)
    y_a = _flash(q_a, bias, k_a, onehot, vt_a, "flash_moba", *tiling)
    y_b = _flash(qn_b, qr_b, kn_b, krx, vt_b, "flash_mla", *tiling)

    mixed = _gated_merge(x2, g_attn, w_g, y_a, y_b,
                         w_branch_a[0].astype(BF16), w_branch_b[0].astype(BF16))
    h = _out_proj(mixed, w_out[0].astype(BF16), x2)
    out = _ffn(h, row(ffn_norm[0]), w_gate[0].astype(BF16), w_up[0].astype(BF16),
               w_down[0].astype(BF16), row(final_norm))
    return out.reshape(1, SEQ, D_MODEL)
```

```python
import functools

import jax
import jax.numpy as jnp
from jax import lax
from jax.experimental import pallas as pl
from jax.experimental.pallas import tpu as pltpu

F32 = jnp.float32
BF16 = jnp.bfloat16

D_MODEL = 2048
SEQ = 16384
HEADS = 8
HEAD_DIM = 128
WIDTH = HEADS * HEAD_DIM
MOBA_BLOCK = 256
MOBA_TOPK = 3
N_BLOCKS = SEQ // MOBA_BLOCK
MOBA_ROPE_HALF = 16
MLA_ROPE_DIM = 64
MLA_ROPE_HALF = 32
MLA_Q_RANK = 512
MLA_KV_RANK = 256
ROPE_THETA = 500000.0
D_FF = 5632
EPS = 1e-6
NEG = -1e30
LANES = 128
SUBLANES = 8
FLASH_TQ = 1024
FLASH_TK = 512
FLASH_CHUNK = 64
FLASH_GROUP = 2
FLASH_QTILES = 2
MERGE_CHUNK = 512

LOG2E = 1.4426950408889634
MOBA_SCALE = HEAD_DIM ** -0.5 * LOG2E
MLA_SCALE = (HEAD_DIM + MLA_ROPE_DIM) ** -0.5 * LOG2E

VMEM_LIMIT = 52 * 1024 * 1024
FLASH_VMEM_LIMIT = 60 * 1024 * 1024


def _params(*sem, vmem_limit=VMEM_LIMIT):
    return pltpu.CompilerParams(dimension_semantics=sem, vmem_limit_bytes=vmem_limit)


def _rmsnorm_bf16(x, g):
    ms = jnp.mean(x * x, axis=-1, keepdims=True)
    return ((x * lax.rsqrt(ms + EPS)) * g).astype(BF16)


def _rope(x, cos, sin, half):
    lane = lax.broadcasted_iota(jnp.int32, x.shape, 1)
    partner = jnp.where(lane < half, pltpu.roll(x, LANES - half, 1), pltpu.roll(x, half, 1))
    return x * cos + partner * sin


def _rope_tables(pos, freqs):
    ang = pos.astype(F32) * freqs
    c = jnp.cos(ang)
    s = jnp.sin(ang)
    lane = lax.broadcasted_iota(jnp.int32, c.shape, 1)
    ha, hb = MOBA_ROPE_HALF, MLA_ROPE_HALF
    cos_a = jnp.where(lane < 2 * ha, c, 1.0)
    sin_a = jnp.where(lane < ha, -s, jnp.where(lane < 2 * ha, s, 0.0))
    c2 = pltpu.roll(c, LANES - 2 * ha, 1)
    s2 = pltpu.roll(s, LANES - 2 * ha, 1)
    cos_b = jnp.where(lane < 2 * hb, c2, 1.0)
    sin_b = jnp.where(lane < hb, -s2, jnp.where(lane < 2 * hb, s2, 0.0))
    return cos_a, sin_a, cos_b, sin_b


def _moba_bias_t(gate, first_query):
    blk = lax.broadcasted_iota(jnp.int32, gate.shape, 0)
    qry = lax.broadcasted_iota(jnp.int32, gate.shape, 1)
    own = (first_query + qry) // MOBA_BLOCK
    past = blk < own
    g = jnp.where(past, gate, -jnp.inf)
    sel = blk == own
    for _ in range(MOBA_TOPK):
        mx = jnp.max(g, axis=0, keepdims=True)
        cand = jnp.where((g == mx) & past, blk, N_BLOCKS)
        idx = jnp.min(cand, axis=0, keepdims=True)
        hit = blk == idx
        sel = sel | hit
        g = jnp.where(hit, -jnp.inf, g)
    bias = jnp.where(sel, 0.0, NEG)
    return jnp.concatenate([bias, jnp.zeros((LANES - N_BLOCKS, gate.shape[1]), F32)], axis=0)


_NT = (((1,), (1,)), ((), ()))


def _head_major(tm):
    return pl.BlockSpec((HEADS, tm, HEAD_DIM), lambda i, *_: (0, i, 0))


def _store_vt(vt_ref, vt):
    for h in range(HEADS):
        vt_ref[h, 0] = vt[h * HEAD_DIM:(h + 1) * HEAD_DIM, :].astype(BF16)


def _resident(shape):
    return pl.BlockSpec(shape, lambda i: (0,) * len(shape), pipeline_mode=pl.Buffered(1))


def _inproj_kernel(x_ref, g_ref, pos_ref, freq_ref, wq_ref, wk_ref, wvt_ref,
                   wc_ref, qg_ref, wuq_ref, kvg_ref, wuk_ref, wuvt_ref,
                   q_ref, k_ref, vta_ref, bias_ref,
                   qn_ref, qr_ref, kn_ref, vt_ref, kr_ref, kmean_ref):
    i = pl.program_id(0)
    tm = x_ref.shape[0]

    @pl.when(i == 0)
    def _():
        kmean_ref[...] = jnp.zeros_like(kmean_ref)

    xn = _rmsnorm_bf16(x_ref[...], g_ref[...])

    k = jnp.dot(xn, wk_ref[...], preferred_element_type=F32)
    cos, sin, cos_b, sin_b = _rope_tables(pos_ref[...], freq_ref[...])
    slot = lax.broadcasted_iota(jnp.int32, (N_BLOCKS, HEAD_DIM), 0)
    for h in range(HEADS):
        sl = slice(h * HEAD_DIM, (h + 1) * HEAD_DIM)
        r = _rope(k[:, sl], cos, sin, MOBA_ROPE_HALF)
        k_ref[h] = r.astype(BF16)
        means = kmean_ref[:, sl]
        for b in range(tm // MOBA_BLOCK):
            rows = r[b * MOBA_BLOCK:(b + 1) * MOBA_BLOCK]
            means = jnp.where(slot == i * (tm // MOBA_BLOCK) + b,
                              jnp.mean(rows, axis=0, keepdims=True), means)
        kmean_ref[:, sl] = means
    q = jnp.dot(xn, wq_ref[...], preferred_element_type=F32)
    for h in range(HEADS):
        sl = slice(h * HEAD_DIM, (h + 1) * HEAD_DIM)
        qh = _rope(q[:, sl], cos, sin, MOBA_ROPE_HALF) * MOBA_SCALE
        q_ref[h] = qh.astype(BF16)
        gate = lax.dot_general(kmean_ref[:, sl], qh, _NT, preferred_element_type=F32,
                               precision=lax.Precision.HIGHEST)
        bias_ref[h] = jnp.transpose(_moba_bias_t(gate, i * tm)).astype(BF16)
    _store_vt(vta_ref, lax.dot_general(wvt_ref[...], xn, _NT, preferred_element_type=F32))

    c = jnp.dot(xn, wc_ref[...], preferred_element_type=F32)
    cos, sin = cos_b, sin_b
    kr = c[:, MLA_Q_RANK + MLA_KV_RANK:]
    kr_ref[...] = _rope(kr, cos, sin, MLA_ROPE_HALF).astype(BF16)

    cq = _rmsnorm_bf16(c[:, :MLA_Q_RANK], qg_ref[...])
    q = jnp.dot(cq, wuq_ref[...], preferred_element_type=F32)
    ckv = _rmsnorm_bf16(c[:, MLA_Q_RANK:MLA_Q_RANK + MLA_KV_RANK], kvg_ref[...])
    kn = jnp.dot(ckv, wuk_ref[...], preferred_element_type=F32)
    for h in range(HEADS):
        sl = slice(h * HEAD_DIM, (h + 1) * HEAD_DIM)
        sl_r = slice(WIDTH + h * HEAD_DIM, WIDTH + (h + 1) * HEAD_DIM)
        qn_ref[h] = (q[:, sl] * MLA_SCALE).astype(BF16)
        qr_ref[h] = (_rope(q[:, sl_r], cos, sin, MLA_ROPE_HALF) * MLA_SCALE).astype(BF16)
        kn_ref[h] = kn[:, sl].astype(BF16)
    _store_vt(vt_ref, lax.dot_general(wuvt_ref[...], ckv, _NT, preferred_element_type=F32))


def _in_proj(x, g, pos, freqs, w_q, w_k, w_vt, w_c, qg, w_uq, kvg, w_uk, w_uvt, tm):
    rows = lambda width: pl.BlockSpec((tm, width), lambda i: (i, 0))
    vt_spec = pl.BlockSpec((HEADS, 1, HEAD_DIM, tm), lambda i: (0, i, 0, 0))
    head_major = jax.ShapeDtypeStruct((HEADS, SEQ, HEAD_DIM), BF16)
    vt_shape = jax.ShapeDtypeStruct((HEADS, SEQ // tm, HEAD_DIM, tm), BF16)
    return pl.pallas_call(
        _inproj_kernel,
        grid=(SEQ // tm,),
        in_specs=[rows(D_MODEL), _resident((1, D_MODEL)), rows(1), _resident((1, LANES)),
                  _resident((D_MODEL, WIDTH)), _resident((D_MODEL, WIDTH)),
                  _resident((WIDTH, D_MODEL)),
                  _resident((D_MODEL, w_c.shape[1])),
                  _resident((1, MLA_Q_RANK)), _resident((MLA_Q_RANK, 2 * WIDTH)),
                  _resident((1, MLA_KV_RANK)), _resident((MLA_KV_RANK, WIDTH)),
                  _resident((WIDTH, MLA_KV_RANK))],
        out_specs=[_head_major(tm), _head_major(tm), vt_spec, _head_major(tm),
                   _head_major(tm), _head_major(tm), _head_major(tm), vt_spec, rows(LANES)],
        out_shape=[head_major, head_major, vt_shape, head_major,
                   head_major, head_major, head_major, vt_shape,
                   jax.ShapeDtypeStruct((SEQ, LANES), BF16)],
        scratch_shapes=[pltpu.VMEM((N_BLOCKS, WIDTH), F32)],
        compiler_params=_params("arbitrary"),
        name="in_proj",
    )(x, g, pos, freqs, w_q, w_k, w_vt, w_c, qg, w_uq, kvg, w_uk, w_uvt)


FLASH_STREAM_SCRATCH = 10


def _flash_kernel(q_ref, qx_ref, k_ref, kx_ref, vt_ref, o_ref, *scratch, tq, tk):
    n_heads, n_qtiles = q_ref.shape[0], q_ref.shape[1] // tq
    assert n_qtiles in (1, 2)
    i0 = pl.program_id(1) * n_qtiles
    streams = [[], []]
    for h in range(n_heads):
        for t in range(n_qtiles):
            rows = pl.ds(t * tq, tq)
            n = len(streams[0]) + len(streams[1])
            streams[t].append(_flash_stream(
                q_ref.at[h, rows], qx_ref.at[h, rows], k_ref.at[h], kx_ref, vt_ref.at[h],
                o_ref.at[h, rows],
                *scratch[n * FLASH_STREAM_SCRATCH:(n + 1) * FLASH_STREAM_SCRATCH],
                q_tile=i0 + t, tq=tq, tk=tk))
    everyone = streams[0] + streams[1]

    def on(group, stage, *args, **kwargs):
        for stream in group:
            getattr(stream, stage)(*args, **kwargs)

    on(everyone, "score_stage", 0, 0, masked=True)
    on(everyone, "score_stage", 1, 1, masked=True)
    on(everyone, "init")
    on(everyone, "exp_stage", 0)

    def step(group, k, a, last=False):
        if not last:
            on(group, "score_stage", k + 2, a)
        on(group, "exp_stage", 1 - a)
        on(group, "value_stage", k, a)

    def pair(kk, carry):
        step(everyone, 2 * kk, 0)
        step(everyone, 2 * kk + 1, 1)
        return carry

    lax.fori_loop(0, i0, pair, 0)
    step(streams[0], 2 * i0, 0, last=True)
    if streams[1]:
        step(streams[1], 2 * i0, 0)
    on(streams[0], "value_stage", 2 * i0 + 1, 1)
    on(streams[0], "finish")
    if streams[1]:
        step(streams[1], 2 * i0 + 1, 1)
        step(streams[1], 2 * i0 + 2, 0, last=True)
        on(streams[1], "value_stage", 2 * i0 + 3, 1)
        on(streams[1], "finish")


class _FlashStream:
    pass


def _flash_stream(q_ref, qx_ref, k_ref, kx_ref, vt_ref, o_ref,
                  s0_ref, s1_ref, p0_ref, p1_ref, acc_ref,
                  m_ref, max0_ref, max1_ref, alpha0_ref, alpha1_ref, *, q_tile, tq, tk):
    i = q_tile
    diag = tq // tk
    assert diag == 2
    qa = jnp.concatenate([q_ref[...], qx_ref[...]], axis=1)

    def tile_of(n):
        return jnp.where(n < diag, i * diag + n, n - diag)

    def scores(n):
        rows = pl.ds(pl.multiple_of(tile_of(n) * tk, tk), tk)
        ka = jnp.concatenate([k_ref[rows, :], kx_ref[rows, :]], axis=1)
        return lax.dot_general(ka, qa, _NT, preferred_element_type=F32)

    def diag_scores(n):
        s = scores(n)
        key = lax.broadcasted_iota(jnp.int32, s.shape, 0) + n * tk
        qry = lax.broadcasted_iota(jnp.int32, s.shape, 1)
        return jnp.where(key <= qry, s, NEG)

    chunks = [slice(c * FLASH_CHUNK, (c + 1) * FLASH_CHUNK) for c in range(tk // FLASH_CHUNK)]

    def fold8(x, op):
        out = x[0:SUBLANES]
        for r in range(1, x.shape[0] // SUBLANES):
            out = op(out, x[r * SUBLANES:(r + 1) * SUBLANES])
        return out

    def rep(x8, rows):
        return jnp.concatenate([x8] * (rows // SUBLANES), axis=0)

    s_slot, p_slot = (s0_ref, s1_ref), (p0_ref, p1_ref)
    max_slot, alpha_slot = (max0_ref, max1_ref), (alpha0_ref, alpha1_ref)
    ones_rows = jnp.ones((acc_ref.shape[0] - HEAD_DIM, tk), BF16)

    def score_stage(n, slot, masked=False):
        val = diag_scores(n) if masked else scores(n)
        s_slot[slot][...] = val
        mx = fold8(val[chunks[0], :], jnp.maximum)
        for c in chunks[1:]:
            mx = jnp.maximum(mx, fold8(val[c, :], jnp.maximum))
        max_slot[slot][...] = jnp.broadcast_to(jnp.max(mx, axis=0, keepdims=True), mx.shape)

    def exp_stage(slot):
        m = m_ref[...]
        m_new = jnp.maximum(m, max_slot[slot][...])
        m_ref[...] = m_new
        alpha_slot[slot][...] = jnp.exp2(m - m_new)
        m_rep = rep(m_new, FLASH_CHUNK)
        for c in chunks:
            shifted = s_slot[slot][c, :] - m_rep
            p_slot[slot][c, :] = jnp.exp2(shifted.astype(BF16))

    def value_stage(n, slot):
        alpha = rep(alpha_slot[slot][...], acc_ref.shape[0])
        vt1 = jnp.concatenate([vt_ref[tile_of(n)], ones_rows], axis=0)
        pv = jnp.dot(vt1, p_slot[slot][...], preferred_element_type=F32)
        acc_ref[...] = alpha * acc_ref[...] + pv

    def init():
        acc_ref[...] = jnp.zeros_like(acc_ref)
        m_ref[...] = jnp.full((SUBLANES, tq), NEG, F32)

    def finish():
        acc = acc_ref[...]
        out_t = acc[:HEAD_DIM] / acc[HEAD_DIM:HEAD_DIM + 1]
        o_ref[...] = jnp.transpose(out_t).astype(o_ref.dtype)

    stream = _FlashStream()
    stream.score_stage, stream.exp_stage, stream.value_stage = score_stage, exp_stage, value_stage
    stream.init, stream.finish = init, finish
    return stream


def _flash(q_arr, qx_arr, k_arr, kx_arr, vt_arr, name, tq, tk, group, qtiles):
    per_group = lambda rows: pl.BlockSpec((group, rows, HEAD_DIM), lambda g, i: (g, i, 0))
    tq_step = tq * qtiles
    whole = lambda shape: pl.BlockSpec(shape, lambda g, i: (g,) + (0,) * (len(shape) - 1),
                                       pipeline_mode=pl.Buffered(1))
    stream_scratch = (
        [pltpu.VMEM((tk, tq), F32)] * 2
        + [pltpu.VMEM((tk, tq), BF16)] * 2
        + [pltpu.VMEM((HEAD_DIM + 2 * SUBLANES, tq), F32)]
        + [pltpu.VMEM((SUBLANES, tq), F32)] * 5)
    assert len(stream_scratch) == FLASH_STREAM_SCRATCH
    return pl.pallas_call(
        functools.partial(_flash_kernel, tq=tq, tk=tk),
        grid=(HEADS // group, SEQ // tq_step),
        in_specs=[per_group(tq_step), per_group(tq_step),
                  whole((group, SEQ, HEAD_DIM)),
                  pl.BlockSpec((SEQ, LANES), lambda g, i: (0, 0), pipeline_mode=pl.Buffered(1)),
                  whole((group, SEQ // tk, HEAD_DIM, tk))],
        out_specs=per_group(tq_step),
        out_shape=jax.ShapeDtypeStruct((HEADS, SEQ, HEAD_DIM), BF16),
        scratch_shapes=stream_scratch * (group * qtiles),
        compiler_params=_params("parallel", "arbitrary", vmem_limit=FLASH_VMEM_LIMIT),
        name=name,
    )(q_arr, qx_arr, k_arr, kx_arr, vt_arr)


def _merge_kernel(x_ref, g_ref, wg_ref, ya_ref, yb_ref, wa_ref, wb_ref, o_ref):
    xn = _rmsnorm_bf16(x_ref[...], g_ref[...])
    ya = jnp.concatenate([ya_ref[h] for h in range(HEADS)], axis=1)
    yb = jnp.concatenate([yb_ref[h] for h in range(HEADS)], axis=1)
    for c in range(D_MODEL // MERGE_CHUNK):
        cols = slice(c * MERGE_CHUNK, (c + 1) * MERGE_CHUNK)
        cols_b = slice(D_MODEL + c * MERGE_CHUNK, D_MODEL + (c + 1) * MERGE_CHUNK)
        gate_a = jax.nn.sigmoid(jnp.dot(xn, wg_ref[:, cols], preferred_element_type=F32))
        gate_b = jax.nn.sigmoid(jnp.dot(xn, wg_ref[:, cols_b], preferred_element_type=F32))
        a = jnp.dot(ya, wa_ref[:, cols], preferred_element_type=F32)
        b = jnp.dot(yb, wb_ref[:, cols], preferred_element_type=F32)
        o_ref[:, cols] = (gate_a * a + gate_b * b).astype(BF16)


def _gated_merge(x, g, w_g, ya, yb, wa, wb, tm=512):
    return pl.pallas_call(
        _merge_kernel,
        grid=(SEQ // tm,),
        in_specs=[pl.BlockSpec((tm, D_MODEL), lambda i: (i, 0)),
                  _resident((1, D_MODEL)), _resident((D_MODEL, 2 * D_MODEL)),
                  _head_major(tm), _head_major(tm),
                  _resident((WIDTH, D_MODEL)), _resident((WIDTH, D_MODEL))],
        out_specs=pl.BlockSpec((tm, D_MODEL), lambda i: (i, 0)),
        out_shape=jax.ShapeDtypeStruct((SEQ, D_MODEL), BF16),
        compiler_params=_params("parallel"),
        name="gated_merge",
    )(x, g, w_g, ya, yb, wa, wb)


def _out_kernel(m_ref, w_ref, x_ref, o_ref):
    o_ref[...] = x_ref[...] + jnp.dot(m_ref[...], w_ref[...], preferred_element_type=F32)


def _out_proj(mixed, w_out, x, tm=512):
    return pl.pallas_call(
        _out_kernel,
        grid=(SEQ // tm,),
        in_specs=[pl.BlockSpec((tm, D_MODEL), lambda i: (i, 0)),
                  _resident((D_MODEL, D_MODEL)),
                  pl.BlockSpec((tm, D_MODEL), lambda i: (i, 0))],
        out_specs=pl.BlockSpec((tm, D_MODEL), lambda i: (i, 0)),
        out_shape=jax.ShapeDtypeStruct((SEQ, D_MODEL), F32),
        compiler_params=_params("parallel"),
        name="out_proj",
    )(mixed, w_out, x)


def _ffn_kernel(h_ref, g_ref, wg_ref, wu_ref, wd_ref, fg_ref, o_ref, hn_ref, acc_ref):
    f = pl.program_id(1)

    @pl.when(f == 0)
    def _():
        hn_ref[...] = _rmsnorm_bf16(h_ref[...], g_ref[...])
        acc_ref[...] = jnp.zeros_like(acc_ref)

    hn = hn_ref[...]
    gate = jnp.dot(hn, wg_ref[...], preferred_element_type=F32)
    up = jnp.dot(hn, wu_ref[...], preferred_element_type=F32)
    act = (jax.nn.silu(gate) * up).astype(BF16)
    acc_ref[...] += jnp.dot(act, wd_ref[...], preferred_element_type=F32)

    @pl.when(f == pl.num_programs(1) - 1)
    def _():
        y = h_ref[...] + acc_ref[...]
        ms = jnp.mean(y * y, axis=-1, keepdims=True)
        o_ref[...] = (y * lax.rsqrt(ms + EPS)) * fg_ref[...]


def _ffn(h, g, wg, wu, wd, fg, tm=512, tf=512):
    return pl.pallas_call(
        _ffn_kernel,
        grid=(SEQ // tm, D_FF // tf),
        in_specs=[pl.BlockSpec((tm, D_MODEL), lambda i, f: (i, 0)),
                  pl.BlockSpec((1, D_MODEL), lambda i, f: (0, 0)),
                  pl.BlockSpec((D_MODEL, tf), lambda i, f: (0, f)),
                  pl.BlockSpec((D_MODEL, tf), lambda i, f: (0, f)),
                  pl.BlockSpec((tf, D_MODEL), lambda i, f: (f, 0)),
                  pl.BlockSpec((1, D_MODEL), lambda i, f: (0, 0))],
        out_specs=pl.BlockSpec((tm, D_MODEL), lambda i, f: (i, 0)),
        out_shape=jax.ShapeDtypeStruct((SEQ, D_MODEL), F32),
        scratch_shapes=[pltpu.VMEM((tm, D_MODEL), BF16),
                        pltpu.VMEM((tm, D_MODEL), F32)],
        compiler_params=_params("parallel", "arbitrary"),
        name="ffn",
    )(h, g, wg, wu, wd, fg)


def _rope_freqs():
    def inv_freq(d):
        return ROPE_THETA ** (-jnp.arange(0, d, 2, dtype=F32) / d)

    fa = inv_freq(2 * MOBA_ROPE_HALF)
    fb = inv_freq(2 * MLA_ROPE_HALF)
    pad = jnp.zeros((LANES - 2 * MOBA_ROPE_HALF - 2 * MLA_ROPE_HALF,), F32)
    return jnp.concatenate([fa, fa, fb, fb, pad]).reshape(1, LANES)


def kernel(x, positions, attn_norm, w_in, q_norm, w_uq, kv_norm, w_ukv, w_branch_a, w_branch_b,
           w_out, ffn_norm, w_gate, w_up, w_down, final_norm):
    assert x.shape == (1, SEQ, D_MODEL) and w_in.shape[0] == 1
    x2 = x.reshape(SEQ, D_MODEL)
    pos = positions.reshape(SEQ, 1)
    row = lambda v: v.reshape(1, -1).astype(F32)

    w = w_in[0]
    c0 = 3 * WIDTH
    c1 = c0 + MLA_Q_RANK + MLA_KV_RANK + MLA_ROPE_DIM
    w_q = w[:, :WIDTH].astype(BF16)
    w_k = w[:, WIDTH:2 * WIDTH].astype(BF16)
    w_vt = w[:, 2 * WIDTH:c0].T.astype(BF16)
    w_c = jnp.pad(w[:, c0:c1], ((0, 0), (0, LANES - MLA_ROPE_DIM))).astype(BF16)
    w_g = w[:, c1:].astype(BF16)
    uq = w_uq[0].reshape(MLA_Q_RANK, HEADS, HEAD_DIM + MLA_ROPE_DIM)
    uq_rope = jnp.pad(uq[:, :, HEAD_DIM:], ((0, 0), (0, 0), (0, LANES - MLA_ROPE_DIM)))
    w_uq2 = jnp.concatenate([uq[:, :, :HEAD_DIM].reshape(MLA_Q_RANK, WIDTH),
                             uq_rope.reshape(MLA_Q_RANK, WIDTH)], axis=1).astype(BF16)
    ukv = w_ukv[0].reshape(MLA_KV_RANK, HEADS, 2 * HEAD_DIM)
    w_uk = ukv[:, :, :HEAD_DIM].reshape(MLA_KV_RANK, WIDTH).astype(BF16)
    w_uvt = ukv[:, :, HEAD_DIM:].reshape(MLA_KV_RANK, WIDTH).T.astype(BF16)

    g_attn = row(attn_norm[0])
    q_a, k_a, vt_a, bias, qn_b, qr_b, kn_b, vt_b, krx = _in_proj(
        x2, g_attn, pos, _rope_freqs(), w_q, w_k, w_vt,
        w_c, row(q_norm[0]), w_uq2, row(kv_norm[0]), w_uk, w_uvt, FLASH_TK)

    blk = jnp.arange(SEQ, dtype=jnp.int32)[:, None] // MOBA_BLOCK
    onehot = (blk == jnp.arange(LANES, dtype=jnp.int32)[None, :]).astype(BF16)

    tiling = (FLASH_TQ, FLASH_TK, FLASH_GROUP, FLASH_QTILES)
    y_a = _flash(q_a, bias, k_a, onehot, vt_a, "flash_moba", *tiling)
    y_b = _flash(qn_b, qr_b, kn_b, krx, vt_b, "flash_mla", *tiling)

    mixed = _gated_merge(x2, g_attn, w_g, y_a, y_b,
                         w_branch_a[0].astype(BF16), w_branch_b[0].astype(BF16))
    h = _out_proj(mixed, w_out[0].astype(BF16), x2)
    out = _ffn(h, row(ffn_norm[0]), w_gate[0].astype(BF16), w_up[0].astype(BF16),
               w_down[0].astype(BF16), row(final_norm))
    return out.reshape(1, SEQ, D_MODEL)
```
